```python
import functools
import jax, jax.numpy as jnp
from jax import lax
import numpy as np

D_MODEL = 1024
BATCH = 2
SEQ = 8192
DEPTH = 4
DEC_BATCH = 128
DEC_SEQ = 1
PAST_LEN = 8192
PAGE_SIZE = 128

N_HEADS_A = 8
N_KV_A = 2
HEAD_DIM_A = 64
GROUP_A = N_HEADS_A // N_KV_A
WINDOW = 128
BLOCK = WINDOW
N_HEADS_R = 4
DK_R = 128
DV_R = 256
CHUNK = 128
D_FF = 2816
ALPHA = (2.0 * DEPTH) ** 0.25
BETA = (8.0 * DEPTH) ** -0.25
LN_EPS = 1e-5
GN_EPS = 1e-6
NEG = -1e30

QA = N_HEADS_A * HEAD_DIM_A
KA = N_KV_A * HEAD_DIM_A
QR = N_HEADS_R * DK_R
VR = N_HEADS_R * DV_R
IN_COLS = QA + 2 * KA + 2 * QR + 2 * VR + 2 * D_MODEL
IN_SPLITS = (QA, QA + KA, QA + 2 * KA, QA + 2 * KA + QR, QA + 2 * KA + 2 * QR,
             QA + 2 * KA + 2 * QR + VR, QA + 2 * KA + 2 * QR + 2 * VR)

kernel_name = 'swa_sink_gqa_retention_macaron_deepnorm'


def layer_norm(x, g, b):
    xf = x.astype(jnp.float32)
    mu = jnp.mean(xf, axis=-1, keepdims=True)
    var = jnp.mean(jnp.square(xf - mu), axis=-1, keepdims=True)
    return ((xf - mu) * lax.rsqrt(var + LN_EPS) * g + b).astype(x.dtype)


def swiglu_half_step(x, w_gu, w_dn, g, b):
    a, u = jnp.split(x @ w_gu, 2, axis=-1)
    return layer_norm(ALPHA * x + 0.5 * ((jax.nn.silu(a) * u) @ w_dn), g, b)


def alibi_slopes():
    return 2.0 ** (-8.0 * jnp.arange(1, N_HEADS_A + 1, dtype=jnp.float32) / N_HEADS_A)


def retention_log_decay():
    return jnp.log1p(-(2.0 ** (-5.0 - jnp.arange(N_HEADS_R, dtype=jnp.float32))))


def sink_attend(q, k, v, dist, valid, sinks):
    s = jnp.einsum('...qkgd,...skd->...kgqs', q.astype(jnp.float32), k.astype(jnp.float32)) * (HEAD_DIM_A ** -0.5)
    slopes = alibi_slopes().reshape(N_KV_A, GROUP_A, 1, 1)
    s = jnp.where(valid, s - slopes * dist, NEG)
    sink = sinks.astype(jnp.float32).reshape(N_KV_A, GROUP_A, 1, 1)
    m = jnp.maximum(jnp.max(s, axis=-1, keepdims=True), sink)
    p = jnp.exp(s - m)
    p = p / (jnp.sum(p, axis=-1, keepdims=True) + jnp.exp(sink - m))
    return jnp.einsum('...kgqs,...skd->...qkgd', p, v.astype(jnp.float32))


def swa_prompt(q, k, v, sinks):
    B, T = q.shape[0], q.shape[1]
    nb = T // BLOCK
    qb = q.reshape(B, nb, BLOCK, N_KV_A, GROUP_A, HEAD_DIM_A)
    kb = k.reshape(B, nb, BLOCK, N_KV_A, HEAD_DIM_A)
    vb = v.reshape(B, nb, BLOCK, N_KV_A, HEAD_DIM_A)

    def with_prev(xb):
        prev = jnp.pad(xb[:, :-1], ((0, 0), (1, 0), (0, 0), (0, 0), (0, 0)))
        return jnp.concatenate([prev, xb], axis=2)

    i = jnp.arange(BLOCK)[:, None]
    j = jnp.arange(2 * BLOCK)[None, :]
    dist = i + BLOCK - j
    blk = jnp.arange(nb)[:, None, None]
    valid = (dist >= 0) & (dist <= WINDOW) & (blk * BLOCK - BLOCK + j >= 0)
    o = sink_attend(qb, with_prev(kb), with_prev(vb), dist.astype(jnp.float32), valid[:, None, None], sinks)
    return o.reshape(B, T, QA).astype(q.dtype), (k[:, -WINDOW:], v[:, -WINDOW:])


def swa_sample(q, k, v, sinks, cache_k, cache_v):
    DB, Tq = q.shape[0], q.shape[1]
    keys = jnp.concatenate([cache_k.astype(k.dtype), k], axis=1)
    vals = jnp.concatenate([cache_v.astype(v.dtype), v], axis=1)
    i = jnp.arange(Tq)[:, None]
    j = jnp.arange(WINDOW + Tq)[None, :]
    dist = i + WINDOW - j
    valid = (dist >= 0) & (dist <= WINDOW)
    o = sink_attend(q.reshape(DB, Tq, N_KV_A, GROUP_A, HEAD_DIM_A), keys, vals,
                    dist.astype(jnp.float32), valid, sinks)
    return o.reshape(DB, Tq, QA).astype(q.dtype), (keys[:, -WINDOW:], vals[:, -WINDOW:])


def retention_chunk(q, k, v, s0):
    lg = retention_log_decay()
    C = q.shape[1]
    pos = jnp.arange(C, dtype=jnp.float32)
    diff = pos[:, None] - pos[None, :]
    decay = jnp.where(diff >= 0, jnp.exp(jnp.maximum(diff, 0.0) * lg[:, None, None]), 0.0)
    qf, kf, vf = q.astype(jnp.float32), k.astype(jnp.float32), v.astype(jnp.float32)
    scores = jnp.einsum('bihd,bjhd->bhij', qf, kf) * decay
    o = jnp.einsum('bhij,bjhe->bihe', scores, vf)
    o = o + jnp.einsum('bihd,bhde->bihe', qf, s0) * jnp.exp((pos + 1.0)[:, None] * lg)[None, :, :, None]
    k_dec = kf * jnp.exp((C - 1.0 - pos)[:, None] * lg)[None, :, :, None]
    s_new = jnp.exp(C * lg)[None, :, None, None] * s0 + jnp.einsum('bjhd,bjhe->bhde', k_dec, vf)
    return o, s_new


def retention_prompt(q, k, v):
    B, T = q.shape[0], q.shape[1]
    nc = T // CHUNK

    def to_chunks(a):
        return jnp.transpose(a.reshape(B, nc, CHUNK, a.shape[2], a.shape[3]), (1, 0, 2, 3, 4))

    def step(s, qkv):
        o, s_new = retention_chunk(qkv[0], qkv[1], qkv[2], s)
        return s_new, o

    s0 = jnp.zeros((B, N_HEADS_R, DK_R, DV_R), jnp.float32)
    s_fin, o = lax.scan(step, s0, (to_chunks(q), to_chunks(k), to_chunks(v)))
    o = jnp.transpose(o, (1, 0, 2, 3, 4)).reshape(B, T, N_HEADS_R, DV_R)
    return o, s_fin


def retention_sample(q, k, v, state):
    return retention_chunk(q, k, v, state.astype(jnp.float32))


def mixing_sublayer(x, w_in, sinks, gn_g, w_br_a, w_br_r, w_o, g, b, attend, retain):
    Bx, T = x.shape[0], x.shape[1]
    z = x @ w_in
    q_a, k_a, v_a, q_r, k_r, v_r, g_r, gate = jnp.split(z, IN_SPLITS, axis=-1)
    o_a, win = attend(q_a.reshape(Bx, T, N_HEADS_A, HEAD_DIM_A),
                      k_a.reshape(Bx, T, N_KV_A, HEAD_DIM_A),
                      v_a.reshape(Bx, T, N_KV_A, HEAD_DIM_A), sinks)
    o_r, s_new = retain(q_r.reshape(Bx, T, N_HEADS_R, DK_R),
                        k_r.reshape(Bx, T, N_HEADS_R, DK_R) * (DK_R ** -0.5),
                        v_r.reshape(Bx, T, N_HEADS_R, DV_R))
    mu = jnp.mean(o_r, axis=-1, keepdims=True)
    var = jnp.mean(jnp.square(o_r - mu), axis=-1, keepdims=True)
    o_r = ((o_r - mu) * lax.rsqrt(var + GN_EPS)).reshape(Bx, T, VR) * gn_g
    o_r = (o_r * jax.nn.silu(g_r.astype(jnp.float32))).astype(x.dtype)
    g_a, g_b = jnp.split(jax.nn.sigmoid(gate), 2, axis=-1)
    merged = g_a * (o_a @ w_br_a) + g_b * (o_r @ w_br_r)
    return layer_norm(ALPHA * x + merged @ w_o, g, b), win, s_new


def setup_inputs(seed: int = 0) -> dict:
    key = jax.random.key(seed)
    ks = jax.random.split(key, 24)
    f32 = jnp.float32

    def nrm(k, shape, scale):
        return jax.random.normal(k, shape, f32) * scale

    col_scale = jnp.concatenate([
        jnp.ones((QA + KA,), f32), jnp.full((KA,), BETA, f32),
        jnp.ones((2 * QR,), f32), jnp.full((VR,), BETA, f32),
        jnp.ones((VR + 2 * D_MODEL,), f32)])
    return {
        'x_prompt': nrm(ks[0], (BATCH, SEQ, D_MODEL), 1.0),
        'x_sample': nrm(ks[1], (DEC_BATCH, DEC_SEQ, D_MODEL), 1.0),
        'cache_win_k': nrm(ks[2], (DEPTH, DEC_BATCH, WINDOW, N_KV_A, HEAD_DIM_A), 1.0),
        'cache_win_v': nrm(ks[3], (DEPTH, DEC_BATCH, WINDOW, N_KV_A, HEAD_DIM_A), BETA),
        'state_ret': nrm(ks[4], (DEPTH, DEC_BATCH, N_HEADS_R, DK_R, DV_R), 0.1),
        'w_ff1_gu': nrm(ks[5], (DEPTH, D_MODEL, 2 * D_FF), BETA * D_MODEL ** -0.5),
        'w_ff1_dn': nrm(ks[6], (DEPTH, D_FF, D_MODEL), BETA * D_FF ** -0.5),
        'ln1_g': 1.0 + nrm(ks[7], (DEPTH, D_MODEL), 0.02),
        'ln1_b': nrm(ks[8], (DEPTH, D_MODEL), 0.02),
        'w_in': nrm(ks[9], (DEPTH, D_MODEL, IN_COLS), D_MODEL ** -0.5) * col_scale,
        'attn_sinks': nrm(ks[10], (DEPTH, N_HEADS_A), 0.5),
        'ret_gn_g': 1.0 + nrm(ks[11], (DEPTH, VR), 0.02),
        'w_br_a': nrm(ks[12], (DEPTH, QA, D_MODEL), BETA * QA ** -0.5),
        'w_br_r': nrm(ks[13], (DEPTH, VR, D_MODEL), BETA * VR ** -0.5),
        'w_o': nrm(ks[14], (DEPTH, D_MODEL, D_MODEL), BETA * D_MODEL ** -0.5),
        'ln2_g': 1.0 + nrm(ks[15], (DEPTH, D_MODEL), 0.02),
        'ln2_b': nrm(ks[16], (DEPTH, D_MODEL), 0.02),
        'w_ff2_gu': nrm(ks[17], (DEPTH, D_MODEL, 2 * D_FF), BETA * D_MODEL ** -0.5),
        'w_ff2_dn': nrm(ks[18], (DEPTH, D_FF, D_MODEL), BETA * D_FF ** -0.5),
        'ln3_g': 1.0 + nrm(ks[19], (DEPTH, D_MODEL), 0.02),
        'ln3_b': nrm(ks[20], (DEPTH, D_MODEL), 0.02),
    }


def reference(x_prompt, x_sample, cache_win_k, cache_win_v, state_ret,
              w_ff1_gu, w_ff1_dn, ln1_g, ln1_b, w_in, attn_sinks, ret_gn_g,
              w_br_a, w_br_r, w_o, ln2_g, ln2_b, w_ff2_gu, w_ff2_dn, ln3_g, ln3_b):
    xp, xs = x_prompt, x_sample
    pk, pv, pr, sk, sv, sr = [], [], [], [], [], []
    for l in range(DEPTH):
        mix_w = (w_in[l], attn_sinks[l], ret_gn_g[l], w_br_a[l], w_br_r[l], w_o[l], ln2_g[l], ln2_b[l])
        xp = swiglu_half_step(xp, w_ff1_gu[l], w_ff1_dn[l], ln1_g[l], ln1_b[l])
        xs = swiglu_half_step(xs, w_ff1_gu[l], w_ff1_dn[l], ln1_g[l], ln1_b[l])
        xp, (k_new, v_new), s_new = mixing_sublayer(xp, *mix_w, swa_prompt, retention_prompt)
        pk.append(k_new)
        pv.append(v_new)
        pr.append(s_new)
        xs, (k_new, v_new), s_new = mixing_sublayer(
            xs, *mix_w,
            functools.partial(swa_sample, cache_k=cache_win_k[l], cache_v=cache_win_v[l]),
            functools.partial(retention_sample, state=state_ret[l]))
        sk.append(k_new)
        sv.append(v_new)
        sr.append(s_new)
        xp = swiglu_half_step(xp, w_ff2_gu[l], w_ff2_dn[l], ln3_g[l], ln3_b[l])
        xs = swiglu_half_step(xs, w_ff2_gu[l], w_ff2_dn[l], ln3_g[l], ln3_b[l])
    return (xp, xs, jnp.stack(pk), jnp.stack(pv), jnp.stack(pr), jnp.stack(sk), jnp.stack(sv), jnp.stack(sr))
```

```python
import functools
import math

import numpy as np
import jax
import jax.numpy as jnp
from jax import lax
from jax.experimental import pallas as pl
from jax.experimental.pallas import tpu as pltpu

D_MODEL = 1024
DEPTH = 4
N_HEADS_A = 8
N_KV_A = 2
HEAD_DIM_A = 64
GROUP_A = N_HEADS_A // N_KV_A
WINDOW = 128
BLOCK = 128
N_HEADS_R = 4
DK_R = 128
DV_R = 256
D_FF = 2816
ALPHA = (2.0 * DEPTH) ** 0.25
LN_EPS = 1e-5
GN_EPS = 1e-6
NEG = -1e30

QA = N_HEADS_A * HEAD_DIM_A
KA = N_KV_A * HEAD_DIM_A
QR = N_HEADS_R * DK_R
VR = N_HEADS_R * DV_R
C_QA, C_KA, C_VA = 0, QA, QA + KA
C_QR = QA + 2 * KA
C_KR = C_QR + QR
C_VR = C_KR + QR
C_GR = C_VR + VR
C_GATE = C_GR + VR
IN_COLS = C_GATE + 2 * D_MODEL

SLOPES = [2.0 ** (-8.0 * (h + 1) / N_HEADS_A) for h in range(N_HEADS_A)]
LOG_GAMMA = [math.log1p(-(2.0 ** (-5.0 - h))) for h in range(N_HEADS_R)]

BF16 = jnp.bfloat16
F32 = jnp.float32

FFN_CHUNK = 256
FFN_TILE = 512
MIX_TILE = 512
SMP_TILE = 8
VMEM_LIMIT = 60 * 1024 * 1024


def _dot(a, b):
    return jnp.dot(a, b, preferred_element_type=F32)


def _dot_nt(a, b):
    return lax.dot_general(a, b, (((1,), (1,)), ((), ())), preferred_element_type=F32)


def _layer_norm(v, g, b):
    mu = jnp.mean(v, axis=-1, keepdims=True)
    d = v - mu
    var = jnp.mean(d * d, axis=-1, keepdims=True)
    return d * lax.rsqrt(var + LN_EPS) * g + b


def _silu(a):
    return a * jax.nn.sigmoid(a)


def _ffn_body(x_ref, wgu_ref, wdn_ref, g_ref, b_ref, o_ref, acc_ref):
    x = x_ref[...]
    xb = x.astype(BF16)
    for j in range(D_FF // FFN_CHUNK):
        lo = j * FFN_CHUNK
        a = _dot(xb, wgu_ref[:, lo:lo + FFN_CHUNK])
        u = _dot(xb, wgu_ref[:, D_FF + lo:D_FF + lo + FFN_CHUNK])
        h = (_silu(a) * u).astype(BF16)
        d = _dot(h, wdn_ref[lo:lo + FFN_CHUNK, :])
        if j == 0:
            acc_ref[...] = d
        else:
            acc_ref[...] += d
    o_ref[...] = _layer_norm(ALPHA * x + 0.5 * acc_ref[...], g_ref[...], b_ref[...])


def _resident(shape, layer):
    nd = len(shape)
    return pl.BlockSpec((None,) + shape, lambda *_: (layer,) + (0,) * nd,
                        pipeline_mode=pl.Buffered(1))


def _ffn(x, wgu, wdn, g, b, layer, tm):
    m = x.shape[0]
    return pl.pallas_call(
        _ffn_body,
        grid=(m // tm,),
        in_specs=[
            pl.BlockSpec((tm, D_MODEL), lambda i: (i, 0)),
            _resident((D_MODEL, 2 * D_FF), layer),
            _resident((D_FF, D_MODEL), layer),
            _resident((1, D_MODEL), layer),
            _resident((1, D_MODEL), layer),
        ],
        out_specs=pl.BlockSpec((tm, D_MODEL), lambda i: (i, 0)),
        out_shape=jax.ShapeDtypeStruct((m, D_MODEL), F32),
        scratch_shapes=[pltpu.VMEM((tm, D_MODEL), F32)],
        compiler_params=pltpu.CompilerParams(
            dimension_semantics=("arbitrary",), vmem_limit_bytes=VMEM_LIMIT),
        name="ffn",
    )(x, wgu, wdn, g, b)


def _mix_prompt_body(sinks_ref, x_ref, win_ref, dec_ref, rowdec_ref, kdec_ref, gn_ref,
                     wa_ref, wr_ref, wo_ref, g_ref, b_ref,
                     y_ref, kwin_ref, vwin_ref, state_ref,
                     qa_s, k0, k1, k2, k3, v0, v1, v2, v3, qr_s, kr_s, vr_s, oa_s, or_s,
                     *, layer):
    t = pl.program_id(1)
    T = MIX_TILE
    kbufs = (k0, k1, k2, k3)
    vbufs = (v0, v1, v2, v3)

    @pl.when(t == 0)
    def _():
        state_ref[...] = jnp.zeros_like(state_ref)
        for buf in kbufs + vbufs:
            buf[0:BLOCK, :] = jnp.zeros((BLOCK, KA), BF16)

    x = x_ref[...]
    xb = x.astype(BF16)

    za = _dot(xb, win_ref[:, C_QA:C_QR])
    qa_s[...] = (za[:, 0:QA] * (HEAD_DIM_A ** -0.5)).astype(BF16)
    ka = za[:, QA:QA + KA]
    va = za[:, QA + KA:QA + 2 * KA]
    kwin_ref[...] = ka[T - WINDOW:, :]
    vwin_ref[...] = va[T - WINDOW:, :]
    lane = lax.broadcasted_iota(jnp.int32, (T, KA), 1)
    low = lane < HEAD_DIM_A
    for src, bufs in ((ka, kbufs), (va, vbufs)):
        rolled = pltpu.roll(src, HEAD_DIM_A, 1)
        bufs[0][BLOCK:, :] = jnp.where(low, src, 0.0).astype(BF16)
        bufs[1][BLOCK:, :] = jnp.where(low, 0.0, rolled).astype(BF16)
        bufs[2][BLOCK:, :] = jnp.where(low, rolled, 0.0).astype(BF16)
        bufs[3][BLOCK:, :] = jnp.where(low, 0.0, src).astype(BF16)

    zr = _dot(xb, win_ref[:, C_QR:C_VR])
    qr_s[...] = zr[:, 0:QR].astype(BF16)
    kr_s[...] = zr[:, QR:2 * QR] * (DK_R ** -0.5)
    vr_s[...] = _dot(xb, win_ref[:, C_VR:C_GR]).astype(BF16)

    qi = lax.broadcasted_iota(jnp.int32, (BLOCK, 2 * BLOCK), 0)
    kj = lax.broadcasted_iota(jnp.int32, (BLOCK, 2 * BLOCK), 1)
    dist_i = qi + BLOCK - kj
    band = (dist_i >= 0) & (dist_i <= WINDOW)
    dist = dist_i.astype(F32)

    def chunk(c, carry):
        r0 = pl.multiple_of(c * BLOCK, BLOCK)
        first = jnp.logical_and(t == 0, c == 0)
        valid = band & (kj >= jnp.where(first, BLOCK, 0))
        for p in range(N_HEADS_A // 2):
            g = p // (GROUP_A // 2)
            q2 = qa_s[pl.ds(r0, BLOCK), 2 * HEAD_DIM_A * p:2 * HEAD_DIM_A * (p + 1)]
            probs = []
            for e in range(2):
                h = 2 * p + e
                kk = kbufs[2 * g + e][pl.ds(r0, 2 * BLOCK), :]
                s = _dot_nt(q2, kk)
                s = jnp.where(valid, s - SLOPES[h] * dist, NEG)
                sink = sinks_ref[layer, h]
                m = jnp.maximum(jnp.max(s, axis=-1, keepdims=True), sink)
                pe = jnp.exp(s - m)
                den = jnp.sum(pe, axis=-1, keepdims=True) + jnp.exp(sink - m)
                probs.append((pe * (1.0 / den)).astype(BF16))
            o2 = (_dot(probs[0], vbufs[2 * g][pl.ds(r0, 2 * BLOCK), :])
                  + _dot(probs[1], vbufs[2 * g + 1][pl.ds(r0, 2 * BLOCK), :]))
            oa_s[pl.ds(r0, BLOCK), 2 * HEAD_DIM_A * p:2 * HEAD_DIM_A * (p + 1)] = o2.astype(BF16)
        for h in range(N_HEADS_R):
            q = qr_s[pl.ds(r0, BLOCK), DK_R * h:DK_R * (h + 1)]
            kf = kr_s[pl.ds(r0, BLOCK), DK_R * h:DK_R * (h + 1)]
            v = vr_s[pl.ds(r0, BLOCK), DV_R * h:DV_R * (h + 1)]
            sc = _dot_nt(q, kf.astype(BF16)) * dec_ref[h]
            s0 = state_ref[h]
            o = _dot(sc.astype(BF16), v) + _dot(q, s0.astype(BF16)) * rowdec_ref[h]
            kd_t = (kf * kdec_ref[h]).T.astype(BF16)
            state_ref[h] = math.exp(BLOCK * LOG_GAMMA[h]) * s0 + _dot(kd_t, v)
            mu = jnp.mean(o, axis=-1, keepdims=True)
            d = o - mu
            var = jnp.mean(d * d, axis=-1, keepdims=True)
            or_s[pl.ds(r0, BLOCK), DV_R * h:DV_R * (h + 1)] = d * lax.rsqrt(var + GN_EPS)
        return carry

    lax.fori_loop(0, T // BLOCK, chunk, 0)

    for buf in kbufs + vbufs:
        buf[0:BLOCK, :] = buf[T:T + BLOCK, :]

    gr = _dot(xb, win_ref[:, C_GR:C_GATE])
    orf = (or_s[...] * gn_ref[...] * _silu(gr)).astype(BF16)
    gate = _dot(xb, win_ref[:, C_GATE:IN_COLS])
    merged = (jax.nn.sigmoid(gate[:, 0:D_MODEL]) * _dot(oa_s[...], wa_ref[...])
              + jax.nn.sigmoid(gate[:, D_MODEL:]) * _dot(orf, wr_ref[...]))
    y = _dot(merged.astype(BF16), wo_ref[...])
    y_ref[...] = _layer_norm(ALPHA * x + y, g_ref[...], b_ref[...])


def _retention_tables():
    pos = np.arange(BLOCK, dtype=np.float64)
    diff = pos[:, None] - pos[None, :]
    lg = np.asarray(LOG_GAMMA)[:, None, None]
    dec = np.where(diff >= 0, np.exp(np.maximum(diff, 0.0) * lg), 0.0)
    rowdec = np.broadcast_to(np.exp((pos + 1.0)[None, :, None] * lg), (N_HEADS_R, BLOCK, DV_R))
    kdec = np.broadcast_to(np.exp((BLOCK - 1.0 - pos)[None, :, None] * lg), (N_HEADS_R, BLOCK, DK_R))
    return (jnp.asarray(dec, F32), jnp.asarray(rowdec, F32), jnp.asarray(kdec, F32))


def _mix_prompt(x, sinks, w_in, gn, wa, wr, wo, g, b, layer):
    B, S, _ = x.shape
    T = MIX_TILE
    dec, rowdec, kdec = _retention_tables()
    full = lambda shape: pl.BlockSpec(shape, lambda *_: (0,) * len(shape),
                                      pipeline_mode=pl.Buffered(1))
    kv_scratch = [pltpu.VMEM((T + BLOCK, KA), BF16) for _ in range(8)]
    return pl.pallas_call(
        functools.partial(_mix_prompt_body, layer=layer),
        grid=(B, S // T),
        in_specs=[
            pl.BlockSpec(memory_space=pltpu.SMEM),
            pl.BlockSpec((None, T, D_MODEL), lambda bi, ti: (bi, ti, 0)),
            _resident((D_MODEL, IN_COLS), layer),
            full((N_HEADS_R, BLOCK, BLOCK)),
            full((N_HEADS_R, BLOCK, DV_R)),
            full((N_HEADS_R, BLOCK, DK_R)),
            _resident((1, VR), layer),
            _resident((QA, D_MODEL), layer),
            _resident((VR, D_MODEL), layer),
            _resident((D_MODEL, D_MODEL), layer),
            _resident((1, D_MODEL), layer),
            _resident((1, D_MODEL), layer),
        ],
        out_specs=[
            pl.BlockSpec((None, T, D_MODEL), lambda bi, ti: (bi, ti, 0)),
            pl.BlockSpec((None, WINDOW, KA), lambda bi, ti: (bi, 0, 0)),
            pl.BlockSpec((None, WINDOW, KA), lambda bi, ti: (bi, 0, 0)),
            pl.BlockSpec((None, N_HEADS_R, DK_R, DV_R), lambda bi, ti: (bi, 0, 0, 0)),
        ],
        out_shape=[
            jax.ShapeDtypeStruct((B, S, D_MODEL), F32),
            jax.ShapeDtypeStruct((B, WINDOW, KA), F32),
            jax.ShapeDtypeStruct((B, WINDOW, KA), F32),
            jax.ShapeDtypeStruct((B, N_HEADS_R, DK_R, DV_R), F32),
        ],
        scratch_shapes=[pltpu.VMEM((T, QA), BF16)] + kv_scratch + [
            pltpu.VMEM((T, QR), BF16),
            pltpu.VMEM((T, QR), F32),
            pltpu.VMEM((T, VR), BF16),
            pltpu.VMEM((T, QA), BF16),
            pltpu.VMEM((T, VR), F32),
        ],
        compiler_params=pltpu.CompilerParams(
            dimension_semantics=("arbitrary", "arbitrary"), vmem_limit_bytes=VMEM_LIMIT),
        name="mix_prompt",
    )(sinks, x, w_in, dec, rowdec, kdec, gn, wa, wr, wo, g, b)


def _head_placement():
    e = np.zeros((QA, N_HEADS_A * KA), np.float32)
    for h in range(N_HEADS_A):
        g = h // GROUP_A
        for d in range(HEAD_DIM_A):
            e[h * HEAD_DIM_A + d, h * KA + g * HEAD_DIM_A + d] = 1.0
    return e


def _smp_proj_body(x_ref, win_ref, e_ref, z_ref, qm_ref):
    z = _dot(x_ref[...].astype(BF16), win_ref[...])
    z_ref[...] = z
    qa = (z[:, 0:QA] * (HEAD_DIM_A ** -0.5)).astype(BF16)
    qm_ref[...] = _dot(qa, e_ref[...])


def _smp_proj(x, w_in, e, layer):
    n = x.shape[0]
    return pl.pallas_call(
        _smp_proj_body,
        grid=(1,),
        in_specs=[
            pl.BlockSpec((n, D_MODEL), lambda i: (0, 0)),
            _resident((D_MODEL, IN_COLS), layer),
            pl.BlockSpec((QA, N_HEADS_A * KA), lambda i: (0, 0)),
        ],
        out_specs=[
            pl.BlockSpec((n, IN_COLS), lambda i: (0, 0)),
            pl.BlockSpec((n, N_HEADS_A * KA), lambda i: (0, 0)),
        ],
        out_shape=[
            jax.ShapeDtypeStruct((n, IN_COLS), F32),
            jax.ShapeDtypeStruct((n, N_HEADS_A * KA), F32),
        ],
        compiler_params=pltpu.CompilerParams(
            dimension_semantics=("arbitrary",), vmem_limit_bytes=VMEM_LIMIT),
        name="smp_proj",
    )(x, w_in, e)


def _smp_core_body(sinks_ref, qm_ref, kvn_ref, ck_ref, cv_ref, zt_ref, vr_ref, st_ref,
                   o3_ref, or_ref, nk_ref, nv_ref, ns_ref, *, layer):
    row8 = lax.broadcasted_iota(jnp.int32, (N_HEADS_A, KA), 0)
    lane8 = lax.broadcasted_iota(jnp.int32, (N_HEADS_A, KA), 1)
    slope = jnp.zeros((N_HEADS_A, KA), F32)
    sinkv = jnp.zeros((N_HEADS_A, KA), F32)
    for h in range(N_HEADS_A):
        slope = jnp.where(row8 == h, SLOPES[h], slope)
        sinkv = jnp.where(row8 == h, sinks_ref[layer, h], sinkv)
    sink = sinkv[:, 0:1]
    bias = slope * (WINDOW - lane8).astype(F32)
    in_group = (lane8 < HEAD_DIM_A) == (row8 < GROUP_A)
    last_row = lax.broadcasted_iota(jnp.int32, (WINDOW, KA), 0) == WINDOW - 1
    zt = zt_ref[...]
    for bl in range(SMP_TILE):
        qm = qm_ref[bl]
        kb = ck_ref[bl]
        vb = cv_ref[bl]
        kn = kvn_ref[bl:bl + 1, 0:KA]
        vn = kvn_ref[bl:bl + 1, KA:2 * KA]
        s = _dot_nt(qm.astype(BF16), kb.astype(BF16)) - bias
        s_new = jnp.sum(qm * kn, axis=-1, keepdims=True)
        m = jnp.maximum(jnp.maximum(jnp.max(s, axis=-1, keepdims=True), s_new), sink)
        p = jnp.exp(s - m)
        p_new = jnp.exp(s_new - m)
        inv = 1.0 / (jnp.sum(p, axis=-1, keepdims=True) + p_new + jnp.exp(sink - m))
        o = _dot((p * inv).astype(BF16), vb.astype(BF16)) + (p_new * inv) * vn
        o3_ref[bl] = jnp.where(in_group, o, 0.0)
        nk_ref[bl] = jnp.where(last_row, kn, pltpu.roll(kb, WINDOW - 1, 0))
        nv_ref[bl] = jnp.where(last_row, vn, pltpu.roll(vb, WINDOW - 1, 0))
        for h in range(N_HEADS_R):
            gamma = math.exp(LOG_GAMMA[h])
            qc = zt[DK_R * h:DK_R * (h + 1), bl:bl + 1]
            kc = zt[QR + DK_R * h:QR + DK_R * (h + 1), bl:bl + 1] * (DK_R ** -0.5)
            s0 = st_ref[bl, h]
            v = vr_ref[bl:bl + 1, DV_R * h:DV_R * (h + 1)]
            sc = jnp.sum(qc * kc, axis=0, keepdims=True)
            o_r = sc * v + jnp.sum(qc * s0, axis=0, keepdims=True) * gamma
            ns_ref[bl, h] = gamma * s0 + kc * v
            or_ref[bl:bl + 1, DV_R * h:DV_R * (h + 1)] = o_r


def _smp_core(sinks, qm3, kvn, ck, cv, zt, vr, st, layer):
    n = qm3.shape[0]
    tb = SMP_TILE
    return pl.pallas_call(
        functools.partial(_smp_core_body, layer=layer),
        grid=(n // tb,),
        in_specs=[
            pl.BlockSpec(memory_space=pltpu.SMEM),
            pl.BlockSpec((tb, N_HEADS_A, KA), lambda i: (i, 0, 0)),
            pl.BlockSpec((tb, 2 * KA), lambda i: (i, 0)),
            pl.BlockSpec((None, tb, WINDOW, KA), lambda i: (layer, i, 0, 0)),
            pl.BlockSpec((None, tb, WINDOW, KA), lambda i: (layer, i, 0, 0)),
            pl.BlockSpec((None, 2 * QR, tb), lambda i: (i, 0, 0)),
            pl.BlockSpec((tb, VR), lambda i: (i, 0)),
            pl.BlockSpec((None, tb, N_HEADS_R, DK_R, DV_R), lambda i: (layer, i, 0, 0, 0)),
        ],
        out_specs=[
            pl.BlockSpec((tb, N_HEADS_A, KA), lambda i: (i, 0, 0)),
            pl.BlockSpec((tb, VR), lambda i: (i, 0)),
            pl.BlockSpec((tb, WINDOW, KA), lambda i: (i, 0, 0)),
            pl.BlockSpec((tb, WINDOW, KA), lambda i: (i, 0, 0)),
            pl.BlockSpec((tb, N_HEADS_R, DK_R, DV_R), lambda i: (i, 0, 0, 0)),
        ],
        out_shape=[
            jax.ShapeDtypeStruct((n, N_HEADS_A, KA), F32),
            jax.ShapeDtypeStruct((n, VR), F32),
            jax.ShapeDtypeStruct((n, WINDOW, KA), F32),
            jax.ShapeDtypeStruct((n, WINDOW, KA), F32),
            jax.ShapeDtypeStruct((n, N_HEADS_R, DK_R, DV_R), F32),
        ],
        compiler_params=pltpu.CompilerParams(
            dimension_semantics=("arbitrary",), vmem_limit_bytes=VMEM_LIMIT),
        name="smp_core",
    )(sinks, qm3, kvn, ck, cv, zt, vr, st)


def _smp_out_body(x_ref, zg_ref, o3_ref, or_ref, et_ref, gn_ref, wa_ref, wr_ref, wo_ref,
                  g_ref, b_ref, y_ref):
    x = x_ref[...]
    oa = _dot(o3_ref[...].astype(BF16), et_ref[...]).astype(BF16)
    gr = zg_ref[:, 0:VR]
    parts = []
    for h in range(N_HEADS_R):
        o = or_ref[:, DV_R * h:DV_R * (h + 1)]
        mu = jnp.mean(o, axis=-1, keepdims=True)
        d = o - mu
        var = jnp.mean(d * d, axis=-1, keepdims=True)
        parts.append(d * lax.rsqrt(var + GN_EPS))
    orn = jnp.concatenate(parts, axis=-1)
    orf = (orn * gn_ref[...] * _silu(gr)).astype(BF16)
    merged = (jax.nn.sigmoid(zg_ref[:, VR:VR + D_MODEL]) * _dot(oa, wa_ref[...])
              + jax.nn.sigmoid(zg_ref[:, VR + D_MODEL:]) * _dot(orf, wr_ref[...]))
    y = _dot(merged.astype(BF16), wo_ref[...])
    y_ref[...] = _layer_norm(ALPHA * x + y, g_ref[...], b_ref[...])


def _smp_out(x, zg, o3, o_r, et, gn, wa, wr, wo, g, b, layer):
    n = x.shape[0]
    whole = lambda a: pl.BlockSpec(a.shape, lambda i: (0,) * a.ndim)
    return pl.pallas_call(
        _smp_out_body,
        grid=(1,),
        in_specs=[
            whole(x), whole(zg), whole(o3), whole(o_r), whole(et),
            _resident((1, VR), layer),
            _resident((QA, D_MODEL), layer),
            _resident((VR, D_MODEL), layer),
            _resident((D_MODEL, D_MODEL), layer),
            _resident((1, D_MODEL), layer),
            _resident((1, D_MODEL), layer),
        ],
        out_specs=pl.BlockSpec((n, D_MODEL), lambda i: (0, 0)),
        out_shape=jax.ShapeDtypeStruct((n, D_MODEL), F32),
        compiler_params=pltpu.CompilerParams(
            dimension_semantics=("arbitrary",), vmem_limit_bytes=VMEM_LIMIT),
        name="smp_out",
    )(x, zg, o3, o_r, et, gn, wa, wr, wo, g, b)


def _mix_sample(x, sinks, w_in, cache_k, cache_v, state, gn, wa, wr, wo, g, b, e, et, layer):
    n = x.shape[0]
    z, qm = _smp_proj(x, w_in, e, layer)
    qm3 = qm.reshape(n, N_HEADS_A, KA)
    kvn = z[:, C_KA:C_QR]
    zt = z[:, C_QR:C_VR].T.reshape(2 * QR, n // SMP_TILE, SMP_TILE).transpose(1, 0, 2)
    vr = z[:, C_VR:C_GR]
    o3, o_r, nk, nv, ns = _smp_core(sinks, qm3, kvn, cache_k, cache_v, zt, vr, state, layer)
    y = _smp_out(x, z[:, C_GR:], o3.reshape(n, N_HEADS_A * KA), o_r, et, gn, wa, wr, wo, g, b, layer)
    return y, nk, nv, ns


def kernel(x_prompt, x_sample, cache_win_k, cache_win_v, state_ret, w_ff1_gu, w_ff1_dn, ln1_g, ln1_b, w_in, attn_sinks, ret_gn_g, w_br_a, w_br_r, w_o, ln2_g, ln2_b, w_ff2_gu, w_ff2_dn, ln3_g, ln3_b):
    B, S, _ = x_prompt.shape
    n = x_sample.shape[0]
    bf = lambda w: w.astype(BF16)
    row = lambda p: p.reshape(DEPTH, 1, -1)
    w1gu, w1dn, w2gu, w2dn = bf(w_ff1_gu), bf(w_ff1_dn), bf(w_ff2_gu), bf(w_ff2_dn)
    win, wa, wr, wo = bf(w_in), bf(w_br_a), bf(w_br_r), bf(w_o)
    g1, b1, g2, b2, g3, b3 = row(ln1_g), row(ln1_b), row(ln2_g), row(ln2_b), row(ln3_g), row(ln3_b)
    gn = row(ret_gn_g)
    e_np = _head_placement()
    e = jnp.asarray(e_np, BF16)
    et = jnp.asarray(e_np.T, BF16)
    ck = cache_win_k.reshape(DEPTH, n, WINDOW, KA)
    cv = cache_win_v.reshape(DEPTH, n, WINDOW, KA)

    xp = x_prompt.reshape(B * S, D_MODEL)
    xs = x_sample.reshape(n, D_MODEL)
    pk, pv, pr, sk, sv, sr = [], [], [], [], [], []
    for l in range(DEPTH):
        xp = _ffn(xp, w1gu, w1dn, g1, b1, l, FFN_TILE)
        xs = _ffn(xs, w1gu, w1dn, g1, b1, l, n)
        yp, kw, vw, st = _mix_prompt(xp.reshape(B, S, D_MODEL), attn_sinks, win, gn, wa, wr, wo,
                                     g2, b2, l)
        xp = yp.reshape(B * S, D_MODEL)
        pk.append(kw)
        pv.append(vw)
        pr.append(st)
        xs, nk, nv, ns = _mix_sample(xs, attn_sinks, win, ck, cv, state_ret, gn, wa, wr, wo,
                                     g2, b2, e, et, l)
        sk.append(nk)
        sv.append(nv)
        sr.append(ns)
        xp = _ffn(xp, w2gu, w2dn, g3, b3, l, FFN_TILE)
        xs = _ffn(xs, w2gu, w2dn, g3, b3, l, n)
    win_shape = (DEPTH, -1, WINDOW, N_KV_A, HEAD_DIM_A)
    return (xp.reshape(B, S, D_MODEL), xs.reshape(n, 1, D_MODEL),
            jnp.stack(pk).reshape(win_shape), jnp.stack(pv).reshape(win_shape), jnp.stack(pr),
            jnp.stack(sk).reshape(win_shape), jnp.stack(sv).reshape(win_shape), jnp.stack(sr))
```

```python
import functools
import math

import numpy as np
import jax
import jax.numpy as jnp
from jax import lax
from jax.experimental import pallas as pl
from jax.experimental.pallas import tpu as pltpu

D_MODEL = 1024
DEPTH = 4
N_HEADS_A = 8
N_KV_A = 2
HEAD_DIM_A = 64
GROUP_A = N_HEADS_A // N_KV_A
WINDOW = 128
BLOCK = 128
N_HEADS_R = 4
DK_R = 128
DV_R = 256
D_FF = 2816
ALPHA = (2.0 * DEPTH) ** 0.25
LN_EPS = 1e-5
GN_EPS = 1e-6
NEG = -1e30

QA = N_HEADS_A * HEAD_DIM_A
KA = N_KV_A * HEAD_DIM_A
QR = N_HEADS_R * DK_R
VR = N_HEADS_R * DV_R
C_QA, C_KA, C_VA = 0, QA, QA + KA
C_QR = QA + 2 * KA
C_KR = C_QR + QR
C_VR = C_KR + QR
C_GR = C_VR + VR
C_GATE = C_GR + VR
IN_COLS = C_GATE + 2 * D_MODEL

SLOPES = [2.0 ** (-8.0 * (h + 1) / N_HEADS_A) for h in range(N_HEADS_A)]
LOG_GAMMA = [math.log1p(-(2.0 ** (-5.0 - h))) for h in range(N_HEADS_R)]

BF16 = jnp.bfloat16
F32 = jnp.float32

FFN_CHUNK = 256
FFN_TILE = 512
MIX_TILE = 512
RET_CHUNK = 256
SMP_TILE = 8
VMEM_LIMIT = 60 * 1024 * 1024


def _dot(a, b):
    return jnp.dot(a, b, preferred_element_type=F32)


def _dot_nt(a, b):
    return lax.dot_general(a, b, (((1,), (1,)), ((), ())), preferred_element_type=F32)


def _layer_norm(v, g, b):
    mu = jnp.mean(v, axis=-1, keepdims=True)
    d = v - mu
    var = jnp.mean(d * d, axis=-1, keepdims=True)
    return d * lax.rsqrt(var + LN_EPS) * g + b


def _silu(a):
    return a * jax.nn.sigmoid(a)


def _ffn_body(x_ref, wgu_ref, wdn_ref, g_ref, b_ref, o_ref, acc_ref):
    x = x_ref[...]
    xb = x.astype(BF16)
    for j in range(D_FF // FFN_CHUNK):
        lo = j * FFN_CHUNK
        a = _dot(xb, wgu_ref[:, lo:lo + FFN_CHUNK])
        u = _dot(xb, wgu_ref[:, D_FF + lo:D_FF + lo + FFN_CHUNK])
        h = (_silu(a) * u).astype(BF16)
        d = _dot(h, wdn_ref[lo:lo + FFN_CHUNK, :])
        if j == 0:
            acc_ref[...] = d
        else:
            acc_ref[...] += d
    o_ref[...] = _layer_norm(ALPHA * x + 0.5 * acc_ref[...], g_ref[...], b_ref[...])


def _resident(shape, layer):
    nd = len(shape)
    return pl.BlockSpec((None,) + shape, lambda *_: (layer,) + (0,) * nd,
                        pipeline_mode=pl.Buffered(1))


def _ffn(x, wgu, wdn, g, b, layer, tm):
    m = x.shape[0]
    return pl.pallas_call(
        _ffn_body,
        grid=(m // tm,),
        in_specs=[
            pl.BlockSpec((tm, D_MODEL), lambda i: (i, 0)),
            _resident((D_MODEL, 2 * D_FF), layer),
            _resident((D_FF, D_MODEL), layer),
            _resident((1, D_MODEL), layer),
            _resident((1, D_MODEL), layer),
        ],
        out_specs=pl.BlockSpec((tm, D_MODEL), lambda i: (i, 0)),
        out_shape=jax.ShapeDtypeStruct((m, D_MODEL), F32),
        scratch_shapes=[pltpu.VMEM((tm, D_MODEL), F32)],
        compiler_params=pltpu.CompilerParams(
            dimension_semantics=("arbitrary",), vmem_limit_bytes=VMEM_LIMIT),
        name="ffn",
    )(x, wgu, wdn, g, b)


def _mix_prompt_body(sinks_ref, x_ref, win_ref, bias_ref, dec_ref, rowdec_ref, kdec_ref, gn_ref,
                     wa_ref, wr_ref, wo_ref, g_ref, b_ref,
                     y_ref, kwin_ref, vwin_ref, state_ref,
                     qa_s, kt0, kt1, kt2, kt3, v0, v1, v2, v3, oa_s, orf_s,
                     *, layer):
    t = pl.program_id(1)
    T = MIX_TILE
    NB = T // BLOCK
    ktbufs = (kt0, kt1, kt2, kt3)
    vbufs = (v0, v1, v2, v3)

    @pl.when(t == 0)
    def _():
        state_ref[...] = jnp.zeros_like(state_ref)
        for buf in ktbufs:
            buf[:, 0:BLOCK] = jnp.zeros((KA, BLOCK), BF16)
        for buf in vbufs:
            buf[0:BLOCK, :] = jnp.zeros((BLOCK, KA), BF16)

    x = x_ref[...]
    xb = x.astype(BF16)

    za = _dot(xb, win_ref[:, C_QA:C_QR])
    q = (za[:, 0:QA] * (HEAD_DIM_A ** -0.5)).astype(BF16)
    for c in range(NB):
        for p in range(N_HEADS_A // 2):
            qa_s[(c * 4 + p) * BLOCK:(c * 4 + p + 1) * BLOCK, :] = (
                q[c * BLOCK:(c + 1) * BLOCK, p * KA:(p + 1) * KA])
    ka = za[:, QA:QA + KA]
    va = za[:, QA + KA:QA + 2 * KA]
    kwin_ref[...] = ka[T - WINDOW:, :]
    vwin_ref[...] = va[T - WINDOW:, :]
    ka_t = ka.T
    ka_tr = pltpu.roll(ka_t, HEAD_DIM_A, 0)
    top = lax.broadcasted_iota(jnp.int32, (KA, T), 0) < HEAD_DIM_A
    kt0[:, BLOCK:] = jnp.where(top, ka_t, 0.0).astype(BF16)
    kt1[:, BLOCK:] = jnp.where(top, 0.0, ka_tr).astype(BF16)
    kt2[:, BLOCK:] = jnp.where(top, ka_tr, 0.0).astype(BF16)
    kt3[:, BLOCK:] = jnp.where(top, 0.0, ka_t).astype(BF16)
    va_r = pltpu.roll(va, HEAD_DIM_A, 1)
    low = lax.broadcasted_iota(jnp.int32, (T, KA), 1) < HEAD_DIM_A
    v0[BLOCK:, :] = jnp.where(low, va, 0.0).astype(BF16)
    v1[BLOCK:, :] = jnp.where(low, 0.0, va_r).astype(BF16)
    v2[BLOCK:, :] = jnp.where(low, va_r, 0.0).astype(BF16)
    v3[BLOCK:, :] = jnp.where(low, 0.0, va).astype(BF16)

    row2 = lax.broadcasted_iota(jnp.int32, (2 * BLOCK, 1), 0) < BLOCK
    sink_col = []
    for g in range(N_KV_A):
        for e in range(2):
            sink_col.append(jnp.where(row2, sinks_ref[layer, 4 * g + e], sinks_ref[layer, 4 * g + 2 + e]))
    low_o = lax.broadcasted_iota(jnp.int32, (2 * BLOCK, KA), 1) < HEAD_DIM_A

    def scores(c):
        out = []
        for g in range(N_KV_A):
            qg = qa_s[(c * 4 + 2 * g) * BLOCK:(c * 4 + 2 * g + 2) * BLOCK, :]
            for e in range(2):
                out.append(_dot(qg, ktbufs[2 * g + e][:, c * BLOCK:(c + 2) * BLOCK]))
        return out

    def softmax(c, s_list):
        probs, invs = [], []
        for ge in range(4):
            if c == 0:
                bias = bias_ref[jnp.where(t == 0, 4, 0) + ge]
            else:
                bias = bias_ref[ge]
            s = s_list[ge] + bias
            m = jnp.maximum(jnp.max(s, axis=-1, keepdims=True), sink_col[ge])
            pe = jnp.exp(s - m)
            den = jnp.sum(pe, axis=-1, keepdims=True) + jnp.exp(sink_col[ge] - m)
            probs.append(pe.astype(BF16))
            invs.append(1.0 / den)
        return probs, invs

    def attend(c, probs, invs):
        for g in range(N_KV_A):
            o = (_dot(probs[2 * g], vbufs[2 * g][c * BLOCK:(c + 2) * BLOCK, :])
                 + _dot(probs[2 * g + 1], vbufs[2 * g + 1][c * BLOCK:(c + 2) * BLOCK, :]))
            o = (o * jnp.where(low_o, invs[2 * g], invs[2 * g + 1])).astype(BF16)
            oa_s[c * BLOCK:(c + 1) * BLOCK, 2 * g * KA:(2 * g + 1) * KA] = o[0:BLOCK]
            oa_s[c * BLOCK:(c + 1) * BLOCK, (2 * g + 1) * KA:(2 * g + 2) * KA] = o[BLOCK:]

    RC = RET_CHUNK
    NR = T // RC
    g_chunk = [math.exp(RC * lg) for lg in LOG_GAMMA]

    def ret_local(zr, vr, cc):
        sc, upd = [], []
        for h in range(N_HEADS_R):
            qh = zr[cc * RC:(cc + 1) * RC, DK_R * h:DK_R * (h + 1)].astype(BF16)
            kf = zr[cc * RC:(cc + 1) * RC, QR + DK_R * h:QR + DK_R * (h + 1)] * (DK_R ** -0.5)
            vh = vr[cc * RC:(cc + 1) * RC, DV_R * h:DV_R * (h + 1)]
            sc.append(_dot_nt(qh, kf.astype(BF16)))
            upd.append(_dot((kf * kdec_ref[h]).T.astype(BF16), vh))
        return sc, upd

    def ret_out(zr, vr, gr, cc, sc, s_in):
        for h in range(N_HEADS_R):
            qh = zr[cc * RC:(cc + 1) * RC, DK_R * h:DK_R * (h + 1)].astype(BF16)
            vh = vr[cc * RC:(cc + 1) * RC, DV_R * h:DV_R * (h + 1)]
            o = (_dot((sc[h] * dec_ref[h]).astype(BF16), vh)
                 + _dot(qh, s_in[h].astype(BF16)) * rowdec_ref[h])
            mu = jnp.mean(o, axis=-1, keepdims=True)
            d = o - mu
            var = jnp.mean(d * d, axis=-1, keepdims=True)
            on = d * lax.rsqrt(var + GN_EPS)
            gate = _silu(gr[cc * RC:(cc + 1) * RC, DV_R * h:DV_R * (h + 1)])
            orf_s[cc * RC:(cc + 1) * RC, DV_R * h:DV_R * (h + 1)] = (
                on * gn_ref[:, DV_R * h:DV_R * (h + 1)] * gate).astype(BF16)

    zr = _dot(xb, win_ref[:, C_QR:C_VR])
    s_cur = scores(0)
    vr = _dot(xb, win_ref[:, C_VR:C_GR]).astype(BF16)
    p_cur = softmax(0, s_cur)
    local = [ret_local(zr, vr, cc) for cc in range(NR)]
    s_nxt = scores(1)
    attend(0, *p_cur)
    gr = _dot(xb, win_ref[:, C_GR:C_GATE])
    p_cur = softmax(1, s_nxt)
    s_nxt = scores(2)
    attend(1, *p_cur)
    states = [[state_ref[h] for h in range(N_HEADS_R)]]
    for cc in range(NR):
        states.append([g_chunk[h] * states[cc][h] + local[cc][1][h] for h in range(N_HEADS_R)])
    for h in range(N_HEADS_R):
        state_ref[h] = states[NR][h]
    ret_out(zr, vr, gr, 0, local[0][0], states[0])
    gate_a = jax.nn.sigmoid(_dot(xb, win_ref[:, C_GATE:C_GATE + D_MODEL]))
    p_cur = softmax(2, s_nxt)
    s_nxt = scores(3)
    attend(2, *p_cur)
    for cc in range(1, NR):
        ret_out(zr, vr, gr, cc, local[cc][0], states[cc])
    gate_r = jax.nn.sigmoid(_dot(xb, win_ref[:, C_GATE + D_MODEL:IN_COLS]))
    p_cur = softmax(3, s_nxt)
    attend(3, *p_cur)

    for buf in ktbufs:
        buf[:, 0:BLOCK] = buf[:, T:T + BLOCK]
    for buf in vbufs:
        buf[0:BLOCK, :] = buf[T:T + BLOCK, :]

    merged = gate_a * _dot(oa_s[...], wa_ref[...]) + gate_r * _dot(orf_s[...], wr_ref[...])
    y = _dot(merged.astype(BF16), wo_ref[...])
    y_ref[...] = _layer_norm(ALPHA * x + y, g_ref[...], b_ref[...])


def _attention_bias():
    i = np.arange(BLOCK)[:, None]
    j = np.arange(2 * BLOCK)[None, :]
    dist = i + BLOCK - j
    band = (dist >= 0) & (dist <= WINDOW)
    out = np.zeros((8, 2 * BLOCK, 2 * BLOCK), np.float32)
    for first in range(2):
        valid = band & ((j >= BLOCK) | (first == 0))
        for g in range(N_KV_A):
            for e in range(2):
                for half in range(2):
                    h = 4 * g + 2 * half + e
                    out[4 * first + 2 * g + e, half * BLOCK:(half + 1) * BLOCK, :] = np.where(
                        valid, -SLOPES[h] * dist, NEG)
    return jnp.asarray(out)


def _retention_tables():
    pos = np.arange(RET_CHUNK, dtype=np.float64)
    diff = pos[:, None] - pos[None, :]
    lg = np.asarray(LOG_GAMMA)[:, None, None]
    dec = np.where(diff >= 0, np.exp(np.maximum(diff, 0.0) * lg), 0.0)
    rowdec = np.broadcast_to(np.exp((pos + 1.0)[None, :, None] * lg), (N_HEADS_R, RET_CHUNK, DV_R))
    kdec = np.broadcast_to(np.exp((RET_CHUNK - 1.0 - pos)[None, :, None] * lg),
                           (N_HEADS_R, RET_CHUNK, DK_R))
    return (jnp.asarray(dec, F32), jnp.asarray(rowdec, F32), jnp.asarray(kdec, F32))


def _mix_prompt(x, sinks, w_in, gn, wa, wr, wo, g, b, layer):
    B, S, _ = x.shape
    T = MIX_TILE
    bias = _attention_bias()
    dec, rowdec, kdec = _retention_tables()
    full = lambda a: pl.BlockSpec(a.shape, lambda *_: (0,) * a.ndim, pipeline_mode=pl.Buffered(1))
    return pl.pallas_call(
        functools.partial(_mix_prompt_body, layer=layer),
        grid=(B, S // T),
        in_specs=[
            pl.BlockSpec(memory_space=pltpu.SMEM),
            pl.BlockSpec((None, T, D_MODEL), lambda bi, ti: (bi, ti, 0)),
            _resident((D_MODEL, IN_COLS), layer),
            full(bias), full(dec), full(rowdec), full(kdec),
            _resident((1, VR), layer),
            _resident((QA, D_MODEL), layer),
            _resident((VR, D_MODEL), layer),
            _resident((D_MODEL, D_MODEL), layer),
            _resident((1, D_MODEL), layer),
            _resident((1, D_MODEL), layer),
        ],
        out_specs=[
            pl.BlockSpec((None, T, D_MODEL), lambda bi, ti: (bi, ti, 0)),
            pl.BlockSpec((None, WINDOW, KA), lambda bi, ti: (bi, 0, 0)),
            pl.BlockSpec((None, WINDOW, KA), lambda bi, ti: (bi, 0, 0)),
            pl.BlockSpec((None, N_HEADS_R, DK_R, DV_R), lambda bi, ti: (bi, 0, 0, 0)),
        ],
        out_shape=[
            jax.ShapeDtypeStruct((B, S, D_MODEL), F32),
            jax.ShapeDtypeStruct((B, WINDOW, KA), F32),
            jax.ShapeDtypeStruct((B, WINDOW, KA), F32),
            jax.ShapeDtypeStruct((B, N_HEADS_R, DK_R, DV_R), F32),
        ],
        scratch_shapes=(
            [pltpu.VMEM((T * 4, KA), BF16)]
            + [pltpu.VMEM((KA, T + BLOCK), BF16) for _ in range(4)]
            + [pltpu.VMEM((T + BLOCK, KA), BF16) for _ in range(4)]
            + [pltpu.VMEM((T, QA), BF16), pltpu.VMEM((T, VR), BF16)]),
        compiler_params=pltpu.CompilerParams(
            dimension_semantics=("arbitrary", "arbitrary"), vmem_limit_bytes=VMEM_LIMIT),
        name="mix_prompt",
    )(sinks, x, w_in, bias, dec, rowdec, kdec, gn, wa, wr, wo, g, b)


def _head_placement():
    e = np.zeros((QA, N_HEADS_A * KA), np.float32)
    for h in range(N_HEADS_A):
        g = h // GROUP_A
        for d in range(HEAD_DIM_A):
            e[h * HEAD_DIM_A + d, h * KA + g * HEAD_DIM_A + d] = 1.0
    return e


def _smp_proj_body(x_ref, win_ref, e_ref, z_ref, qm_ref):
    z = _dot(x_ref[...].astype(BF16), win_ref[...])
    z_ref[...] = z
    qa = (z[:, 0:QA] * (HEAD_DIM_A ** -0.5)).astype(BF16)
    qm_ref[...] = _dot(qa, e_ref[...])


def _smp_proj(x, w_in, e, layer):
    n = x.shape[0]
    return pl.pallas_call(
        _smp_proj_body,
        grid=(1,),
        in_specs=[
            pl.BlockSpec((n, D_MODEL), lambda i: (0, 0)),
            _resident((D_MODEL, IN_COLS), layer),
            pl.BlockSpec((QA, N_HEADS_A * KA), lambda i: (0, 0)),
        ],
        out_specs=[
            pl.BlockSpec((n, IN_COLS), lambda i: (0, 0)),
            pl.BlockSpec((n, N_HEADS_A * KA), lambda i: (0, 0)),
        ],
        out_shape=[
            jax.ShapeDtypeStruct((n, IN_COLS), F32),
            jax.ShapeDtypeStruct((n, N_HEADS_A * KA), F32),
        ],
        compiler_params=pltpu.CompilerParams(
            dimension_semantics=("arbitrary",), vmem_limit_bytes=VMEM_LIMIT),
        name="smp_proj",
    )(x, w_in, e)


def _smp_core_body(sinks_ref, qm_ref, kvn_ref, ck_ref, cv_ref, zt_ref, vr_ref, st_ref,
                   o3_ref, or_ref, nk_ref, nv_ref, ns_ref, *, layer):
    row8 = lax.broadcasted_iota(jnp.int32, (N_HEADS_A, KA), 0)
    lane8 = lax.broadcasted_iota(jnp.int32, (N_HEADS_A, KA), 1)
    slope = jnp.zeros((N_HEADS_A, KA), F32)
    sinkv = jnp.zeros((N_HEADS_A, KA), F32)
    for h in range(N_HEADS_A):
        slope = jnp.where(row8 == h, SLOPES[h], slope)
        sinkv = jnp.where(row8 == h, sinks_ref[layer, h], sinkv)
    sink = sinkv[:, 0:1]
    bias = slope * (WINDOW - lane8).astype(F32)
    in_group = (lane8 < HEAD_DIM_A) == (row8 < GROUP_A)
    last_row = lax.broadcasted_iota(jnp.int32, (WINDOW, KA), 0) == WINDOW - 1
    zt = zt_ref[...]
    for bl in range(SMP_TILE):
        qm = qm_ref[bl]
        kb = ck_ref[bl]
        vb = cv_ref[bl]
        kn = kvn_ref[bl:bl + 1, 0:KA]
        vn = kvn_ref[bl:bl + 1, KA:2 * KA]
        s = _dot_nt(qm.astype(BF16), kb.astype(BF16)) - bias
        s_new = jnp.sum(qm * kn, axis=-1, keepdims=True)
        m = jnp.maximum(jnp.maximum(jnp.max(s, axis=-1, keepdims=True), s_new), sink)
        p = jnp.exp(s - m)
        p_new = jnp.exp(s_new - m)
        inv = 1.0 / (jnp.sum(p, axis=-1, keepdims=True) + p_new + jnp.exp(sink - m))
        o = _dot((p * inv).astype(BF16), vb.astype(BF16)) + (p_new * inv) * vn
        o3_ref[bl] = jnp.where(in_group, o, 0.0)
        nk_ref[bl] = jnp.where(last_row, kn, pltpu.roll(kb, WINDOW - 1, 0))
        nv_ref[bl] = jnp.where(last_row, vn, pltpu.roll(vb, WINDOW - 1, 0))
        for h in range(N_HEADS_R):
            gamma = math.exp(LOG_GAMMA[h])
            qc = zt[DK_R * h:DK_R * (h + 1), bl:bl + 1]
            kc = zt[QR + DK_R * h:QR + DK_R * (h + 1), bl:bl + 1] * (DK_R ** -0.5)
            s0 = st_ref[bl, h]
            v = vr_ref[bl:bl + 1, DV_R * h:DV_R * (h + 1)]
            sc = jnp.sum(qc * kc, axis=0, keepdims=True)
            o_r = sc * v + jnp.sum(qc * s0, axis=0, keepdims=True) * gamma
            ns_ref[bl, h] = gamma * s0 + kc * v
            or_ref[bl:bl + 1, DV_R * h:DV_R * (h + 1)] = o_r


def _smp_core(sinks, qm3, kvn, ck, cv, zt, vr, st, layer):
    n = qm3.shape[0]
    tb = SMP_TILE
    return pl.pallas_call(
        functools.partial(_smp_core_body, layer=layer),
        grid=(n // tb,),
        in_specs=[
            pl.BlockSpec(memory_space=pltpu.SMEM),
            pl.BlockSpec((tb, N_HEADS_A, KA), lambda i: (i, 0, 0)),
            pl.BlockSpec((tb, 2 * KA), lambda i: (i, 0)),
            pl.BlockSpec((None, tb, WINDOW, KA), lambda i: (layer, i, 0, 0)),
            pl.BlockSpec((None, tb, WINDOW, KA), lambda i: (layer, i, 0, 0)),
            pl.BlockSpec((None, 2 * QR, tb), lambda i: (i, 0, 0)),
            pl.BlockSpec((tb, VR), lambda i: (i, 0)),
            pl.BlockSpec((None, tb, N_HEADS_R, DK_R, DV_R), lambda i: (layer, i, 0, 0, 0)),
        ],
        out_specs=[
            pl.BlockSpec((tb, N_HEADS_A, KA), lambda i: (i, 0, 0)),
            pl.BlockSpec((tb, VR), lambda i: (i, 0)),
            pl.BlockSpec((tb, WINDOW, KA), lambda i: (i, 0, 0)),
            pl.BlockSpec((tb, WINDOW, KA), lambda i: (i, 0, 0)),
            pl.BlockSpec((tb, N_HEADS_R, DK_R, DV_R), lambda i: (i, 0, 0, 0)),
        ],
        out_shape=[
            jax.ShapeDtypeStruct((n, N_HEADS_A, KA), F32),
            jax.ShapeDtypeStruct((n, VR), F32),
            jax.ShapeDtypeStruct((n, WINDOW, KA), F32),
            jax.ShapeDtypeStruct((n, WINDOW, KA), F32),
            jax.ShapeDtypeStruct((n, N_HEADS_R, DK_R, DV_R), F32),
        ],
        compiler_params=pltpu.CompilerParams(
            dimension_semantics=("arbitrary",), vmem_limit_bytes=VMEM_LIMIT),
        name="smp_core",
    )(sinks, qm3, kvn, ck, cv, zt, vr, st)


def _smp_out_body(x_ref, zg_ref, o3_ref, or_ref, et_ref, gn_ref, wa_ref, wr_ref, wo_ref,
                  g_ref, b_ref, y_ref):
    x = x_ref[...]
    oa = _dot(o3_ref[...].astype(BF16), et_ref[...]).astype(BF16)
    gr = zg_ref[:, 0:VR]
    parts = []
    for h in range(N_HEADS_R):
        o = or_ref[:, DV_R * h:DV_R * (h + 1)]
        mu = jnp.mean(o, axis=-1, keepdims=True)
        d = o - mu
        var = jnp.mean(d * d, axis=-1, keepdims=True)
        parts.append(d * lax.rsqrt(var + GN_EPS))
    orn = jnp.concatenate(parts, axis=-1)
    orf = (orn * gn_ref[...] * _silu(gr)).astype(BF16)
    merged = (jax.nn.sigmoid(zg_ref[:, VR:VR + D_MODEL]) * _dot(oa, wa_ref[...])
              + jax.nn.sigmoid(zg_ref[:, VR + D_MODEL:]) * _dot(orf, wr_ref[...]))
    y = _dot(merged.astype(BF16), wo_ref[...])
    y_ref[...] = _layer_norm(ALPHA * x + y, g_ref[...], b_ref[...])


def _smp_out(x, zg, o3, o_r, et, gn, wa, wr, wo, g, b, layer):
    n = x.shape[0]
    whole = lambda a: pl.BlockSpec(a.shape, lambda i: (0,) * a.ndim)
    return pl.pallas_call(
        _smp_out_body,
        grid=(1,),
        in_specs=[
            whole(x), whole(zg), whole(o3), whole(o_r), whole(et),
            _resident((1, VR), layer),
            _resident((QA, D_MODEL), layer),
            _resident((VR, D_MODEL), layer),
            _resident((D_MODEL, D_MODEL), layer),
            _resident((1, D_MODEL), layer),
            _resident((1, D_MODEL), layer),
        ],
        out_specs=pl.BlockSpec((n, D_MODEL), lambda i: (0, 0)),
        out_shape=jax.ShapeDtypeStruct((n, D_MODEL), F32),
        compiler_params=pltpu.CompilerParams(
            dimension_semantics=("arbitrary",), vmem_limit_bytes=VMEM_LIMIT),
        name="smp_out",
    )(x, zg, o3, o_r, et, gn, wa, wr, wo, g, b)


def _mix_sample(x, sinks, w_in, cache_k, cache_v, state, gn, wa, wr, wo, g, b, e, et, layer):
    n = x.shape[0]
    z, qm = _smp_proj(x, w_in, e, layer)
    qm3 = qm.reshape(n, N_HEADS_A, KA)
    kvn = z[:, C_KA:C_QR]
    zt = z[:, C_QR:C_VR].T.reshape(2 * QR, n // SMP_TILE, SMP_TILE).transpose(1, 0, 2)
    vr = z[:, C_VR:C_GR]
    o3, o_r, nk, nv, ns = _smp_core(sinks, qm3, kvn, cache_k, cache_v, zt, vr, state, layer)
    y = _smp_out(x, z[:, C_GR:], o3.reshape(n, N_HEADS_A * KA), o_r, et, gn, wa, wr, wo, g, b, layer)
    return y, nk, nv, ns


def kernel(x_prompt, x_sample, cache_win_k, cache_win_v, state_ret, w_ff1_gu, w_ff1_dn, ln1_g, ln1_b, w_in, attn_sinks, ret_gn_g, w_br_a, w_br_r, w_o, ln2_g, ln2_b, w_ff2_gu, w_ff2_dn, ln3_g, ln3_b):
    B, S, _ = x_prompt.shape
    n = x_sample.shape[0]
    bf = lambda w: w.astype(BF16)
    row = lambda p: p.reshape(DEPTH, 1, -1)
    w1gu, w1dn, w2gu, w2dn = bf(w_ff1_gu), bf(w_ff1_dn), bf(w_ff2_gu), bf(w_ff2_dn)
    win, wa, wr, wo = bf(w_in), bf(w_br_a), bf(w_br_r), bf(w_o)
    g1, b1, g2, b2, g3, b3 = row(ln1_g), row(ln1_b), row(ln2_g), row(ln2_b), row(ln3_g), row(ln3_b)
    gn = row(ret_gn_g)
    e_np = _head_placement()
    e = jnp.asarray(e_np, BF16)
    et = jnp.asarray(e_np.T, BF16)
    ck = cache_win_k.reshape(DEPTH, n, WINDOW, KA)
    cv = cache_win_v.reshape(DEPTH, n, WINDOW, KA)

    xp = x_prompt.reshape(B * S, D_MODEL)
    xs = x_sample.reshape(n, D_MODEL)
    pk, pv, pr, sk, sv, sr = [], [], [], [], [], []
    for l in range(DEPTH):
        xp = _ffn(xp, w1gu, w1dn, g1, b1, l, FFN_TILE)
        xs = _ffn(xs, w1gu, w1dn, g1, b1, l, n)
        yp, kw, vw, st = _mix_prompt(xp.reshape(B, S, D_MODEL), attn_sinks, win, gn, wa, wr, wo,
                                     g2, b2, l)
        xp = yp.reshape(B * S, D_MODEL)
        pk.append(kw)
        pv.append(vw)
        pr.append(st)
        xs, nk, nv, ns = _mix_sample(xs, attn_sinks, win, ck, cv, state_ret, gn, wa, wr, wo,
                                     g2, b2, e, et, l)
        sk.append(nk)
        sv.append(nv)
        sr.append(ns)
        xp = _ffn(xp, w2gu, w2dn, g3, b3, l, FFN_TILE)
        xs = _ffn(xs, w2gu, w2dn, g3, b3, l, n)
    win_shape = (DEPTH, -1, WINDOW, N_KV_A, HEAD_DIM_A)
    return (xp.reshape(B, S, D_MODEL), xs.reshape(n, 1, D_MODEL),
            jnp.stack(pk).reshape(win_shape), jnp.stack(pv).reshape(win_shape), jnp.stack(pr),
            jnp.stack(sk).reshape(win_shape), jnp.stack(sv).reshape(win_shape), jnp.stack(sr))
```

```python
import functools
import math

import numpy as np
import jax
import jax.numpy as jnp
from jax import lax
from jax.experimental import pallas as pl
from jax.experimental.pallas import tpu as pltpu

D_MODEL = 1024
DEPTH = 4
N_HEADS_A = 8
N_KV_A = 2
HEAD_DIM_A = 64
GROUP_A = N_HEADS_A // N_KV_A
WINDOW = 128
BLOCK = 128
N_HEADS_R = 4
DK_R = 128
DV_R = 256
D_FF = 2816
ALPHA = (2.0 * DEPTH) ** 0.25
LN_EPS = 1e-5
GN_EPS = 1e-6
NEG = -1e30

QA = N_HEADS_A * HEAD_DIM_A
KA = N_KV_A * HEAD_DIM_A
QR = N_HEADS_R * DK_R
VR = N_HEADS_R * DV_R
C_QA, C_KA, C_VA = 0, QA, QA + KA
C_QR = QA + 2 * KA
C_KR = C_QR + QR
C_VR = C_KR + QR
C_GR = C_VR + VR
C_GATE = C_GR + VR
IN_COLS = C_GATE + 2 * D_MODEL

SLOPES = [2.0 ** (-8.0 * (h + 1) / N_HEADS_A) for h in range(N_HEADS_A)]
LOG_GAMMA = [math.log1p(-(2.0 ** (-5.0 - h))) for h in range(N_HEADS_R)]

BF16 = jnp.bfloat16
F32 = jnp.float32

FFN_CHUNK = 256
FFN_TILE = 1024
FFN_SUBTILE = 512
MIX_TILE = 512
RET_CHUNK = 256
SMP_TILE = 8
VMEM_LIMIT = 60 * 1024 * 1024


def _dot(a, b):
    return jnp.dot(a, b, preferred_element_type=F32)


def _dot_nt(a, b):
    return lax.dot_general(a, b, (((1,), (1,)), ((), ())), preferred_element_type=F32)


def _layer_norm(v, g, b):
    mu = jnp.mean(v, axis=-1, keepdims=True)
    d = v - mu
    var = jnp.mean(d * d, axis=-1, keepdims=True)
    return d * lax.rsqrt(var + LN_EPS) * g + b


def _silu(a):
    return a * jax.nn.sigmoid(a)


def _ffn_body(x_ref, wgu_ref, wdn_ref, g_ref, b_ref, o_ref, acc_ref):
    sub = min(FFN_SUBTILE, x_ref.shape[0])
    for s in range(x_ref.shape[0] // sub):
        rows = slice(s * sub, (s + 1) * sub)
        x = x_ref[rows, :]
        xb = x.astype(BF16)
        for j in range(D_FF // FFN_CHUNK):
            lo = j * FFN_CHUNK
            a = _dot(xb, wgu_ref[:, lo:lo + FFN_CHUNK])
            u = _dot(xb, wgu_ref[:, D_FF + lo:D_FF + lo + FFN_CHUNK])
            h = (_silu(a) * u).astype(BF16)
            d = _dot(h, wdn_ref[lo:lo + FFN_CHUNK, :])
            if j == 0:
                acc_ref[rows, :] = d
            else:
                acc_ref[rows, :] += d
        o_ref[rows, :] = _layer_norm(ALPHA * x + 0.5 * acc_ref[rows, :], g_ref[...], b_ref[...])


def _resident(shape, layer):
    nd = len(shape)
    return pl.BlockSpec((None,) + shape, lambda *_: (layer,) + (0,) * nd,
                        pipeline_mode=pl.Buffered(1))


def _ffn(x, wgu, wdn, g, b, layer, tm):
    m = x.shape[0]
    return pl.pallas_call(
        _ffn_body,
        grid=(m // tm,),
        in_specs=[
            pl.BlockSpec((tm, D_MODEL), lambda i: (i, 0)),
            _resident((D_MODEL, 2 * D_FF), layer),
            _resident((D_FF, D_MODEL), layer),
            _resident((1, D_MODEL), layer),
            _resident((1, D_MODEL), layer),
        ],
        out_specs=pl.BlockSpec((tm, D_MODEL), lambda i: (i, 0)),
        out_shape=jax.ShapeDtypeStruct((m, D_MODEL), F32),
        scratch_shapes=[pltpu.VMEM((tm, D_MODEL), F32)],
        compiler_params=pltpu.CompilerParams(
            dimension_semantics=("arbitrary",), vmem_limit_bytes=VMEM_LIMIT),
        name="ffn",
    )(x, wgu, wdn, g, b)


def _mix_prompt_body(sinks_ref, x_ref, win_ref, bias_ref, dec_ref, rowdec_ref, kdec_ref, gn_ref,
                     wa_ref, wr_ref, wo_ref, g_ref, b_ref,
                     y_ref, kwin_ref, vwin_ref, state_ref,
                     qa_s, kt0, kt1, kt2, kt3, v0, v1, v2, v3, oa_s, orf_s,
                     *, layer):
    t = pl.program_id(1)
    T = MIX_TILE
    NB = T // BLOCK
    ktbufs = (kt0, kt1, kt2, kt3)
    vbufs = (v0, v1, v2, v3)

    @pl.when(t == 0)
    def _():
        state_ref[...] = jnp.zeros_like(state_ref)
        for buf in ktbufs:
            buf[:, 0:BLOCK] = jnp.zeros((KA, BLOCK), BF16)
        for buf in vbufs:
            buf[0:BLOCK, :] = jnp.zeros((BLOCK, KA), BF16)

    x = x_ref[...]
    xb = x.astype(BF16)

    za = _dot(xb, win_ref[:, C_QA:C_QR])
    q = (za[:, 0:QA] * (HEAD_DIM_A ** -0.5)).astype(BF16)
    for c in range(NB):
        for p in range(N_HEADS_A // 2):
            qa_s[(c * 4 + p) * BLOCK:(c * 4 + p + 1) * BLOCK, :] = (
                q[c * BLOCK:(c + 1) * BLOCK, p * KA:(p + 1) * KA])
    ka = za[:, QA:QA + KA]
    va = za[:, QA + KA:QA + 2 * KA]
    kwin_ref[...] = ka[T - WINDOW:, :]
    vwin_ref[...] = va[T - WINDOW:, :]
    ka_t = ka.T
    ka_tr = pltpu.roll(ka_t, HEAD_DIM_A, 0)
    top = lax.broadcasted_iota(jnp.int32, (KA, T), 0) < HEAD_DIM_A
    kt0[:, BLOCK:] = jnp.where(top, ka_t, 0.0).astype(BF16)
    kt1[:, BLOCK:] = jnp.where(top, 0.0, ka_tr).astype(BF16)
    kt2[:, BLOCK:] = jnp.where(top, ka_tr, 0.0).astype(BF16)
    kt3[:, BLOCK:] = jnp.where(top, 0.0, ka_t).astype(BF16)
    va_r = pltpu.roll(va, HEAD_DIM_A, 1)
    low = lax.broadcasted_iota(jnp.int32, (T, KA), 1) < HEAD_DIM_A
    v0[BLOCK:, :] = jnp.where(low, va, 0.0).astype(BF16)
    v1[BLOCK:, :] = jnp.where(low, 0.0, va_r).astype(BF16)
    v2[BLOCK:, :] = jnp.where(low, va_r, 0.0).astype(BF16)
    v3[BLOCK:, :] = jnp.where(low, 0.0, va).astype(BF16)

    row2 = lax.broadcasted_iota(jnp.int32, (2 * BLOCK, 1), 0) < BLOCK
    sink_col = []
    for g in range(N_KV_A):
        for e in range(2):
            sink_col.append(jnp.where(row2, sinks_ref[layer, 4 * g + e], sinks_ref[layer, 4 * g + 2 + e]))
    low_o = lax.broadcasted_iota(jnp.int32, (2 * BLOCK, KA), 1) < HEAD_DIM_A

    def scores(c):
        out = []
        for g in range(N_KV_A):
            qg = qa_s[(c * 4 + 2 * g) * BLOCK:(c * 4 + 2 * g + 2) * BLOCK, :]
            for e in range(2):
                out.append(_dot(qg, ktbufs[2 * g + e][:, c * BLOCK:(c + 2) * BLOCK]))
        return out

    def softmax(c, s_list):
        probs, invs = [], []
        for ge in range(4):
            if c == 0:
                bias = bias_ref[jnp.where(t == 0, 4, 0) + ge]
            else:
                bias = bias_ref[ge]
            s = s_list[ge] + bias
            m = jnp.maximum(jnp.max(s, axis=-1, keepdims=True), sink_col[ge])
            pe = jnp.exp(s - m)
            den = jnp.sum(pe, axis=-1, keepdims=True) + jnp.exp(sink_col[ge] - m)
            probs.append(pe.astype(BF16))
            invs.append(1.0 / den)
        return probs, invs

    def attend(c, probs, invs):
        for g in range(N_KV_A):
            o = (_dot(probs[2 * g], vbufs[2 * g][c * BLOCK:(c + 2) * BLOCK, :])
                 + _dot(probs[2 * g + 1], vbufs[2 * g + 1][c * BLOCK:(c + 2) * BLOCK, :]))
            o = (o * jnp.where(low_o, invs[2 * g], invs[2 * g + 1])).astype(BF16)
            oa_s[c * BLOCK:(c + 1) * BLOCK, 2 * g * KA:(2 * g + 1) * KA] = o[0:BLOCK]
            oa_s[c * BLOCK:(c + 1) * BLOCK, (2 * g + 1) * KA:(2 * g + 2) * KA] = o[BLOCK:]

    RC = RET_CHUNK
    NR = T // RC
    g_chunk = [math.exp(RC * lg) for lg in LOG_GAMMA]

    def ret_local(zr, vr, cc):
        sc, upd = [], []
        for h in range(N_HEADS_R):
            qh = zr[cc * RC:(cc + 1) * RC, DK_R * h:DK_R * (h + 1)].astype(BF16)
            kf = zr[cc * RC:(cc + 1) * RC, QR + DK_R * h:QR + DK_R * (h + 1)] * (DK_R ** -0.5)
            vh = vr[cc * RC:(cc + 1) * RC, DV_R * h:DV_R * (h + 1)]
            sc.append(_dot_nt(qh, kf.astype(BF16)))
            upd.append(_dot((kf * kdec_ref[h]).T.astype(BF16), vh))
        return sc, upd

    def ret_out(zr, vr, gr, cc, sc, s_in):
        for h in range(N_HEADS_R):
            qh = zr[cc * RC:(cc + 1) * RC, DK_R * h:DK_R * (h + 1)].astype(BF16)
            vh = vr[cc * RC:(cc + 1) * RC, DV_R * h:DV_R * (h + 1)]
            o = (_dot((sc[h] * dec_ref[h]).astype(BF16), vh)
                 + _dot(qh, s_in[h].astype(BF16)) * rowdec_ref[h])
            mu = jnp.mean(o, axis=-1, keepdims=True)
            d = o - mu
            var = jnp.mean(d * d, axis=-1, keepdims=True)
            on = d * lax.rsqrt(var + GN_EPS)
            gate = _silu(gr[cc * RC:(cc + 1) * RC, DV_R * h:DV_R * (h + 1)])
            orf_s[cc * RC:(cc + 1) * RC, DV_R * h:DV_R * (h + 1)] = (
                on * gn_ref[:, DV_R * h:DV_R * (h + 1)] * gate).astype(BF16)

    zr = _dot(xb, win_ref[:, C_QR:C_VR])
    s_cur = scores(0)
    vr = _dot(xb, win_ref[:, C_VR:C_GR]).astype(BF16)
    p_cur = softmax(0, s_cur)
    local = [ret_local(zr, vr, cc) for cc in range(NR)]
    s_nxt = scores(1)
    attend(0, *p_cur)
    gr = _dot(xb, win_ref[:, C_GR:C_GATE])
    p_cur = softmax(1, s_nxt)
    s_nxt = scores(2)
    attend(1, *p_cur)
    states = [[state_ref[h] for h in range(N_HEADS_R)]]
    for cc in range(NR):
        states.append([g_chunk[h] * states[cc][h] + local[cc][1][h] for h in range(N_HEADS_R)])
    for h in range(N_HEADS_R):
        state_ref[h] = states[NR][h]
    ret_out(zr, vr, gr, 0, local[0][0], states[0])
    gate_a = jax.nn.sigmoid(_dot(xb, win_ref[:, C_GATE:C_GATE + D_MODEL]))
    p_cur = softmax(2, s_nxt)
    s_nxt = scores(3)
    attend(2, *p_cur)
    for cc in range(1, NR):
        ret_out(zr, vr, gr, cc, local[cc][0], states[cc])
    gate_r = jax.nn.sigmoid(_dot(xb, win_ref[:, C_GATE + D_MODEL:IN_COLS]))
    p_cur = softmax(3, s_nxt)
    attend(3, *p_cur)

    for buf in ktbufs:
        buf[:, 0:BLOCK] = buf[:, T:T + BLOCK]
    for buf in vbufs:
        buf[0:BLOCK, :] = buf[T:T + BLOCK, :]

    merged = gate_a * _dot(oa_s[...], wa_ref[...]) + gate_r * _dot(orf_s[...], wr_ref[...])
    y = _dot(merged.astype(BF16), wo_ref[...])
    y_ref[...] = _layer_norm(ALPHA * x + y, g_ref[...], b_ref[...])


def _attention_bias():
    i = np.arange(BLOCK)[:, None]
    j = np.arange(2 * BLOCK)[None, :]
    dist = i + BLOCK - j
    band = (dist >= 0) & (dist <= WINDOW)
    out = np.zeros((8, 2 * BLOCK, 2 * BLOCK), np.float32)
    for first in range(2):
        valid = band & ((j >= BLOCK) | (first == 0))
        for g in range(N_KV_A):
            for e in range(2):
                for half in range(2):
                    h = 4 * g + 2 * half + e
                    out[4 * first + 2 * g + e, half * BLOCK:(half + 1) * BLOCK, :] = np.where(
                        valid, -SLOPES[h] * dist, NEG)
    return jnp.asarray(out)


def _retention_tables():
    pos = np.arange(RET_CHUNK, dtype=np.float64)
    diff = pos[:, None] - pos[None, :]
    lg = np.asarray(LOG_GAMMA)[:, None, None]
    dec = np.where(diff >= 0, np.exp(np.maximum(diff, 0.0) * lg), 0.0)
    rowdec = np.broadcast_to(np.exp((pos + 1.0)[None, :, None] * lg), (N_HEADS_R, RET_CHUNK, DV_R))
    kdec = np.broadcast_to(np.exp((RET_CHUNK - 1.0 - pos)[None, :, None] * lg),
                           (N_HEADS_R, RET_CHUNK, DK_R))
    return (jnp.asarray(dec, F32), jnp.asarray(rowdec, F32), jnp.asarray(kdec, F32))


def _mix_prompt(x, sinks, w_in, gn, wa, wr, wo, g, b, layer):
    B, S, _ = x.shape
    T = MIX_TILE
    bias = _attention_bias()
    dec, rowdec, kdec = _retention_tables()
    full = lambda a: pl.BlockSpec(a.shape, lambda *_: (0,) * a.ndim, pipeline_mode=pl.Buffered(1))
    return pl.pallas_call(
        functools.partial(_mix_prompt_body, layer=layer),
        grid=(B, S // T),
        in_specs=[
            pl.BlockSpec(memory_space=pltpu.SMEM),
            pl.BlockSpec((None, T, D_MODEL), lambda bi, ti: (bi, ti, 0)),
            _resident((D_MODEL, IN_COLS), layer),
            full(bias), full(dec), full(rowdec), full(kdec),
            _resident((1, VR), layer),
            _resident((QA, D_MODEL), layer),
            _resident((VR, D_MODEL), layer),
            _resident((D_MODEL, D_MODEL), layer),
            _resident((1, D_MODEL), layer),
            _resident((1, D_MODEL), layer),
        ],
        out_specs=[
            pl.BlockSpec((None, T, D_MODEL), lambda bi, ti: (bi, ti, 0)),
            pl.BlockSpec((None, WINDOW, KA), lambda bi, ti: (bi, 0, 0)),
            pl.BlockSpec((None, WINDOW, KA), lambda bi, ti: (bi, 0, 0)),
            pl.BlockSpec((None, N_HEADS_R, DK_R, DV_R), lambda bi, ti: (bi, 0, 0, 0)),
        ],
        out_shape=[
            jax.ShapeDtypeStruct((B, S, D_MODEL), F32),
            jax.ShapeDtypeStruct((B, WINDOW, KA), F32),
            jax.ShapeDtypeStruct((B, WINDOW, KA), F32),
            jax.ShapeDtypeStruct((B, N_HEADS_R, DK_R, DV_R), F32),
        ],
        scratch_shapes=(
            [pltpu.VMEM((T * 4, KA), BF16)]
            + [pltpu.VMEM((KA, T + BLOCK), BF16) for _ in range(4)]
            + [pltpu.VMEM((T + BLOCK, KA), BF16) for _ in range(4)]
            + [pltpu.VMEM((T, QA), BF16), pltpu.VMEM((T, VR), BF16)]),
        compiler_params=pltpu.CompilerParams(
            dimension_semantics=("arbitrary", "arbitrary"), vmem_limit_bytes=VMEM_LIMIT),
        name="mix_prompt",
    )(sinks, x, w_in, bias, dec, rowdec, kdec, gn, wa, wr, wo, g, b)


def _head_placement():
    e = np.zeros((QA, N_HEADS_A * KA), np.float32)
    for h in range(N_HEADS_A):
        g = h // GROUP_A
        for d in range(HEAD_DIM_A):
            e[h * HEAD_DIM_A + d, h * KA + g * HEAD_DIM_A + d] = 1.0
    return e


def _smp_proj_body(x_ref, win_ref, e_ref, z_ref, qm_ref):
    z = _dot(x_ref[...].astype(BF16), win_ref[...])
    z_ref[...] = z
    qa = (z[:, 0:QA] * (HEAD_DIM_A ** -0.5)).astype(BF16)
    qm_ref[...] = _dot(qa, e_ref[...])


def _smp_proj(x, w_in, e, layer):
    n = x.shape[0]
    return pl.pallas_call(
        _smp_proj_body,
        grid=(1,),
        in_specs=[
            pl.BlockSpec((n, D_MODEL), lambda i: (0, 0)),
            _resident((D_MODEL, IN_COLS), layer),
            pl.BlockSpec((QA, N_HEADS_A * KA), lambda i: (0, 0)),
        ],
        out_specs=[
            pl.BlockSpec((n, IN_COLS), lambda i: (0, 0)),
            pl.BlockSpec((n, N_HEADS_A * KA), lambda i: (0, 0)),
        ],
        out_shape=[
            jax.ShapeDtypeStruct((n, IN_COLS), F32),
            jax.ShapeDtypeStruct((n, N_HEADS_A * KA), F32),
        ],
        compiler_params=pltpu.CompilerParams(
            dimension_semantics=("arbitrary",), vmem_limit_bytes=VMEM_LIMIT),
        name="smp_proj",
    )(x, w_in, e)


def _smp_core_body(sinks_ref, qm_ref, kvn_ref, ck_ref, cv_ref, zt_ref, vr_ref, st_ref, *rest, layer):
    o3_ref, or_ref, nk_ref, nv_ref, ns_ref = rest[-5:]
    row8 = lax.broadcasted_iota(jnp.int32, (N_HEADS_A, KA), 0)
    lane8 = lax.broadcasted_iota(jnp.int32, (N_HEADS_A, KA), 1)
    slope = jnp.zeros((N_HEADS_A, KA), F32)
    sinkv = jnp.zeros((N_HEADS_A, KA), F32)
    for h in range(N_HEADS_A):
        slope = jnp.where(row8 == h, SLOPES[h], slope)
        sinkv = jnp.where(row8 == h, sinks_ref[layer, h], sinkv)
    sink = sinkv[:, 0:1]
    bias = slope * (WINDOW - lane8).astype(F32)
    in_group = (lane8 < HEAD_DIM_A) == (row8 < GROUP_A)
    last_row = lax.broadcasted_iota(jnp.int32, (WINDOW, KA), 0) == WINDOW - 1
    zt = zt_ref[...]
    scores = [_dot_nt(qm_ref[bl].astype(BF16), ck_ref[bl].astype(BF16)) for bl in range(SMP_TILE)]
    probs = []
    for bl in range(SMP_TILE):
        qm = qm_ref[bl]
        kn = kvn_ref[bl:bl + 1, 0:KA]
        s = scores[bl] - bias
        s_new = jnp.sum(qm * kn, axis=-1, keepdims=True)
        m = jnp.maximum(jnp.maximum(jnp.max(s, axis=-1, keepdims=True), s_new), sink)
        p = jnp.exp(s - m)
        p_new = jnp.exp(s_new - m)
        inv = 1.0 / (jnp.sum(p, axis=-1, keepdims=True) + p_new + jnp.exp(sink - m))
        probs.append(((p * inv).astype(BF16), p_new * inv))
    for bl in range(SMP_TILE):
        kb = ck_ref[bl]
        vb = cv_ref[bl]
        kn = kvn_ref[bl:bl + 1, 0:KA]
        vn = kvn_ref[bl:bl + 1, KA:2 * KA]
        o = _dot(probs[bl][0], vb.astype(BF16)) + probs[bl][1] * vn
        o3_ref[bl] = jnp.where(in_group, o, 0.0)
        nk_ref[bl] = jnp.where(last_row, kn, pltpu.roll(kb, WINDOW - 1, 0))
        nv_ref[bl] = jnp.where(last_row, vn, pltpu.roll(vb, WINDOW - 1, 0))
    for bl in range(SMP_TILE):
        for h in range(N_HEADS_R):
            gamma = math.exp(LOG_GAMMA[h])
            qc = zt[DK_R * h:DK_R * (h + 1), bl:bl + 1]
            kc = zt[QR + DK_R * h:QR + DK_R * (h + 1), bl:bl + 1] * (DK_R ** -0.5)
            s0 = st_ref[bl, h]
            v = vr_ref[bl:bl + 1, DV_R * h:DV_R * (h + 1)]
            sc = jnp.sum(qc * kc, axis=0, keepdims=True)
            o_r = sc * v + jnp.sum(qc * s0, axis=0, keepdims=True) * gamma
            ns_ref[bl, h] = gamma * s0 + kc * v
            or_ref[bl:bl + 1, DV_R * h:DV_R * (h + 1)] = o_r


def _smp_core(sinks, qm3, kvn, ck, cv, zt, vr, st, stacked, layer):
    n = qm3.shape[0]
    tb = SMP_TILE
    in_specs = [
        pl.BlockSpec(memory_space=pltpu.SMEM),
        pl.BlockSpec((tb, N_HEADS_A, KA), lambda i: (i, 0, 0)),
        pl.BlockSpec((tb, 2 * KA), lambda i: (i, 0)),
        pl.BlockSpec((None, tb, WINDOW, KA), lambda i: (layer, i, 0, 0)),
        pl.BlockSpec((None, tb, WINDOW, KA), lambda i: (layer, i, 0, 0)),
        pl.BlockSpec((None, 2 * QR, tb), lambda i: (i, 0, 0)),
        pl.BlockSpec((tb, VR), lambda i: (i, 0)),
        pl.BlockSpec((None, tb, N_HEADS_R, DK_R, DV_R), lambda i: (layer, i, 0, 0, 0)),
    ]
    args = [sinks, qm3, kvn, ck, cv, zt, vr, st]
    aliases = {}
    if stacked is not None:
        for k, buf in enumerate(stacked):
            aliases[len(args)] = 2 + k
            in_specs.append(pl.BlockSpec(memory_space=pl.ANY))
            args.append(buf)
    return pl.pallas_call(
        functools.partial(_smp_core_body, layer=layer),
        grid=(n // tb,),
        in_specs=in_specs,
        out_specs=[
            pl.BlockSpec((tb, N_HEADS_A, KA), lambda i: (i, 0, 0)),
            pl.BlockSpec((tb, VR), lambda i: (i, 0)),
            pl.BlockSpec((None, tb, WINDOW, KA), lambda i: (layer, i, 0, 0)),
            pl.BlockSpec((None, tb, WINDOW, KA), lambda i: (layer, i, 0, 0)),
            pl.BlockSpec((None, tb, N_HEADS_R, DK_R, DV_R), lambda i: (layer, i, 0, 0, 0)),
        ],
        out_shape=[
            jax.ShapeDtypeStruct((n, N_HEADS_A, KA), F32),
            jax.ShapeDtypeStruct((n, VR), F32),
            jax.ShapeDtypeStruct((DEPTH, n, WINDOW, KA), F32),
            jax.ShapeDtypeStruct((DEPTH, n, WINDOW, KA), F32),
            jax.ShapeDtypeStruct((DEPTH, n, N_HEADS_R, DK_R, DV_R), F32),
        ],
        input_output_aliases=aliases,
        compiler_params=pltpu.CompilerParams(
            dimension_semantics=("arbitrary",), vmem_limit_bytes=VMEM_LIMIT),
        name="smp_core",
    )(*args)


def _smp_out_body(x_ref, zg_ref, o3_ref, or_ref, et_ref, gn_ref, wa_ref, wr_ref, wo_ref,
                  g_ref, b_ref, y_ref):
    x = x_ref[...]
    oa = _dot(o3_ref[...].astype(BF16), et_ref[...]).astype(BF16)
    gr = zg_ref[:, 0:VR]
    parts = []
    for h in range(N_HEADS_R):
        o = or_ref[:, DV_R * h:DV_R * (h + 1)]
        mu = jnp.mean(o, axis=-1, keepdims=True)
        d = o - mu
        var = jnp.mean(d * d, axis=-1, keepdims=True)
        parts.append(d * lax.rsqrt(var + GN_EPS))
    orn = jnp.concatenate(parts, axis=-1)
    orf = (orn * gn_ref[...] * _silu(gr)).astype(BF16)
    merged = (jax.nn.sigmoid(zg_ref[:, VR:VR + D_MODEL]) * _dot(oa, wa_ref[...])
              + jax.nn.sigmoid(zg_ref[:, VR + D_MODEL:]) * _dot(orf, wr_ref[...]))
    y = _dot(merged.astype(BF16), wo_ref[...])
    y_ref[...] = _layer_norm(ALPHA * x + y, g_ref[...], b_ref[...])


def _smp_out(x, zg, o3, o_r, et, gn, wa, wr, wo, g, b, layer):
    n = x.shape[0]
    whole = lambda a: pl.BlockSpec(a.shape, lambda i: (0,) * a.ndim)
    return pl.pallas_call(
        _smp_out_body,
        grid=(1,),
        in_specs=[
            whole(x), whole(zg), whole(o3), whole(o_r), whole(et),
            _resident((1, VR), layer),
            _resident((QA, D_MODEL), layer),
            _resident((VR, D_MODEL), layer),
            _resident((D_MODEL, D_MODEL), layer),
            _resident((1, D_MODEL), layer),
            _resident((1, D_MODEL), layer),
        ],
        out_specs=pl.BlockSpec((n, D_MODEL), lambda i: (0, 0)),
        out_shape=jax.ShapeDtypeStruct((n, D_MODEL), F32),
        compiler_params=pltpu.CompilerParams(
            dimension_semantics=("arbitrary",), vmem_limit_bytes=VMEM_LIMIT),
        name="smp_out",
    )(x, zg, o3, o_r, et, gn, wa, wr, wo, g, b)


def _mix_sample(x, sinks, w_in, cache_k, cache_v, state, stacked, gn, wa, wr, wo, g, b, e, et, layer):
    n = x.shape[0]
    z, qm = _smp_proj(x, w_in, e, layer)
    qm3 = qm.reshape(n, N_HEADS_A, KA)
    kvn = z[:, C_KA:C_QR]
    zt = z[:, C_QR:C_VR].T.reshape(2 * QR, n // SMP_TILE, SMP_TILE).transpose(1, 0, 2)
    vr = z[:, C_VR:C_GR]
    o3, o_r, nk, nv, ns = _smp_core(sinks, qm3, kvn, cache_k, cache_v, zt, vr, state, stacked, layer)
    y = _smp_out(x, z[:, C_GR:], o3.reshape(n, N_HEADS_A * KA), o_r, et, gn, wa, wr, wo, g, b, layer)
    return y, (nk, nv, ns)


def kernel(x_prompt, x_sample, cache_win_k, cache_win_v, state_ret, w_ff1_gu, w_ff1_dn, ln1_g, ln1_b, w_in, attn_sinks, ret_gn_g, w_br_a, w_br_r, w_o, ln2_g, ln2_b, w_ff2_gu, w_ff2_dn, ln3_g, ln3_b):
    B, S, _ = x_prompt.shape
    n = x_sample.shape[0]
    bf = lambda w: w.astype(BF16)
    row = lambda p: p.reshape(DEPTH, 1, -1)
    w1gu, w1dn, w2gu, w2dn = bf(w_ff1_gu), bf(w_ff1_dn), bf(w_ff2_gu), bf(w_ff2_dn)
    win, wa, wr, wo = bf(w_in), bf(w_br_a), bf(w_br_r), bf(w_o)
    g1, b1, g2, b2, g3, b3 = row(ln1_g), row(ln1_b), row(ln2_g), row(ln2_b), row(ln3_g), row(ln3_b)
    gn = row(ret_gn_g)
    e_np = _head_placement()
    e = jnp.asarray(e_np, BF16)
    et = jnp.asarray(e_np.T, BF16)
    ck = cache_win_k.reshape(DEPTH, n, WINDOW, KA)
    cv = cache_win_v.reshape(DEPTH, n, WINDOW, KA)

    xp = x_prompt.reshape(B * S, D_MODEL)
    xs = x_sample.reshape(n, D_MODEL)
    pk, pv, pr = [], [], []
    stacked = None
    for l in range(DEPTH):
        xp = _ffn(xp, w1gu, w1dn, g1, b1, l, FFN_TILE)
        xs = _ffn(xs, w1gu, w1dn, g1, b1, l, n)
        yp, kw, vw, st = _mix_prompt(xp.reshape(B, S, D_MODEL), attn_sinks, win, gn, wa, wr, wo,
                                     g2, b2, l)
        xp = yp.reshape(B * S, D_MODEL)
        pk.append(kw)
        pv.append(vw)
        pr.append(st)
        xs, stacked = _mix_sample(xs, attn_sinks, win, ck, cv, state_ret, stacked, gn, wa, wr, wo,
                                  g2, b2, e, et, l)
        xp = _ffn(xp, w2gu, w2dn, g3, b3, l, FFN_TILE)
        xs = _ffn(xs, w2gu, w2dn, g3, b3, l, n)
    win_shape = (DEPTH, -1, WINDOW, N_KV_A, HEAD_DIM_A)
    return (xp.reshape(B, S, D_MODEL), xs.reshape(n, 1, D_MODEL),
            jnp.stack(pk).reshape(win_shape), jnp.stack(pv).reshape(win_shape), jnp.stack(pr),
            stacked[0].reshape(win_shape), stacked[1].reshape(win_shape), stacked[2])
```

```python
import functools
import math

import numpy as np
import jax
import jax.numpy as jnp
from jax import lax
from jax.experimental import pallas as pl
from jax.experimental.pallas import tpu as pltpu

D_MODEL = 1024
DEPTH = 4
N_HEADS_A = 8
N_KV_A = 2
HEAD_DIM_A = 64
GROUP_A = N_HEADS_A // N_KV_A
WINDOW = 128
BLOCK = 128
N_HEADS_R = 4
DK_R = 128
DV_R = 256
D_FF = 2816
ALPHA = (2.0 * DEPTH) ** 0.25
LN_EPS = 1e-5
GN_EPS = 1e-6
NEG = -1e30

QA = N_HEADS_A * HEAD_DIM_A
KA = N_KV_A * HEAD_DIM_A
QR = N_HEADS_R * DK_R
VR = N_HEADS_R * DV_R
C_QA, C_KA, C_VA = 0, QA, QA + KA
C_QR = QA + 2 * KA
C_KR = C_QR + QR
C_VR = C_KR + QR
C_GR = C_VR + VR
C_GATE = C_GR + VR
IN_COLS = C_GATE + 2 * D_MODEL

SLOPES = [2.0 ** (-8.0 * (h + 1) / N_HEADS_A) for h in range(N_HEADS_A)]
LOG_GAMMA = [math.log1p(-(2.0 ** (-5.0 - h))) for h in range(N_HEADS_R)]

BF16 = jnp.bfloat16
F32 = jnp.float32

FFN_CHUNK = 256
FFN_TILE = 1024
FFN_SUBTILE = 512
MIX_TILE = 512
RET_CHUNK = 256
SMP_TILE = 8
VMEM_LIMIT = 60 * 1024 * 1024


def _dot(a, b):
    return jnp.dot(a, b, preferred_element_type=F32)


def _dot_nt(a, b):
    return lax.dot_general(a, b, (((1,), (1,)), ((), ())), preferred_element_type=F32)


def _layer_norm(v, g, b):
    mu = jnp.mean(v, axis=-1, keepdims=True)
    d = v - mu
    var = jnp.mean(d * d, axis=-1, keepdims=True)
    return d * lax.rsqrt(var + LN_EPS) * g + b


def _silu(a):
    return a * jax.nn.sigmoid(a)


def _cast_side(refs, n_side):
    for k in range(n_side):
        refs[n_side + k][...] = refs[k][...].astype(BF16)


BF16_SUBLANES = 16


def _side(w, layer, total):
    r = w.shape[1]
    nblk = total
    while r % nblk or (r // nblk) % BF16_SUBLANES:
        nblk //= 2
    return (w, layer, r // nblk)


def _side_specs(side, step, total):
    in_specs, out_specs, out_shape, args = [], [], [], []
    for w, layer, rows in side:
        _, r, c = w.shape
        nblk = r // rows
        assert r % rows == 0 and total % nblk == 0
        idx = lambda *g, nblk=nblk: step(*g) // (total // nblk)
        in_specs.append(pl.BlockSpec((None, rows, c), lambda *g, idx=idx, layer=layer: (layer, idx(*g), 0)))
        out_specs.append(pl.BlockSpec((rows, c), lambda *g, idx=idx: (idx(*g), 0)))
        out_shape.append(jax.ShapeDtypeStruct((r, c), BF16))
        args.append(w)
    return in_specs, out_specs, out_shape, args


def _ffn_body(x_ref, wgu_ref, wdn_ref, g_ref, b_ref, *rest, n_side):
    side, (o_ref,), (acc_ref,) = rest[:n_side], rest[n_side:n_side + 1], rest[2 * n_side + 1:]
    _cast_side(side + rest[n_side + 1:2 * n_side + 1], n_side)
    sub = min(FFN_SUBTILE, x_ref.shape[0])
    for s in range(x_ref.shape[0] // sub):
        rows = slice(s * sub, (s + 1) * sub)
        x = x_ref[rows, :]
        xb = x.astype(BF16)
        for j in range(D_FF // FFN_CHUNK):
            lo = j * FFN_CHUNK
            a = _dot(xb, wgu_ref[:, lo:lo + FFN_CHUNK])
            u = _dot(xb, wgu_ref[:, D_FF + lo:D_FF + lo + FFN_CHUNK])
            h = (_silu(a) * u).astype(BF16)
            d = _dot(h, wdn_ref[lo:lo + FFN_CHUNK, :])
            if j == 0:
                acc_ref[rows, :] = d
            else:
                acc_ref[rows, :] += d
        o_ref[rows, :] = _layer_norm(ALPHA * x + 0.5 * acc_ref[rows, :], g_ref[...], b_ref[...])


def _resident(shape, layer):
    nd = len(shape)
    return pl.BlockSpec((None,) + shape, lambda *_: (layer,) + (0,) * nd,
                        pipeline_mode=pl.Buffered(1))


def _whole(a):
    return pl.BlockSpec(a.shape, lambda *_: (0,) * a.ndim, pipeline_mode=pl.Buffered(1))


def _ffn(x, wgu, wdn, g, b, layer, tm, side=()):
    m = x.shape[0]
    steps = m // tm
    s_in, s_out, s_shape, s_args = _side_specs(side, lambda i: i, steps)
    outs = pl.pallas_call(
        functools.partial(_ffn_body, n_side=len(side)),
        grid=(steps,),
        in_specs=[
            pl.BlockSpec((tm, D_MODEL), lambda i: (i, 0)),
            _whole(wgu),
            _whole(wdn),
            _resident((1, D_MODEL), layer),
            _resident((1, D_MODEL), layer),
        ] + s_in,
        out_specs=[pl.BlockSpec((tm, D_MODEL), lambda i: (i, 0))] + s_out,
        out_shape=[jax.ShapeDtypeStruct((m, D_MODEL), F32)] + s_shape,
        scratch_shapes=[pltpu.VMEM((tm, D_MODEL), F32)],
        compiler_params=pltpu.CompilerParams(
            dimension_semantics=("arbitrary",), vmem_limit_bytes=VMEM_LIMIT),
        name="ffn",
    )(x, wgu, wdn, g, b, *s_args)
    return outs[0], outs[1:]


def _mix_prompt_body(sinks_ref, x_ref, win_ref, bias_ref, dec_ref, rowdec_ref, kdec_ref, gn_ref,
                     wa_ref, wr_ref, wo_ref, g_ref, b_ref, *rest, layer, n_side):
    side_in, rest = rest[:n_side], rest[n_side:]
    (y_ref, kwin_ref, vwin_ref, state_ref), rest = rest[:4], rest[4:]
    side_out, rest = rest[:n_side], rest[n_side:]
    qa_s, kt0, kt1, kt2, kt3, v0, v1, v2, v3, oa_s, orf_s = rest
    _mix_prompt_tile(sinks_ref, x_ref, win_ref, bias_ref, dec_ref, rowdec_ref, kdec_ref, gn_ref,
                     wa_ref, wr_ref, wo_ref, g_ref, b_ref, y_ref, kwin_ref, vwin_ref, state_ref,
                     qa_s, kt0, kt1, kt2, kt3, v0, v1, v2, v3, oa_s, orf_s, layer=layer)
    _cast_side(side_in + side_out, n_side)


def _mix_prompt_tile(sinks_ref, x_ref, win_ref, bias_ref, dec_ref, rowdec_ref, kdec_ref, gn_ref,
                     wa_ref, wr_ref, wo_ref, g_ref, b_ref,
                     y_ref, kwin_ref, vwin_ref, state_ref,
                     qa_s, kt0, kt1, kt2, kt3, v0, v1, v2, v3, oa_s, orf_s,
                     *, layer):
    t = pl.program_id(1)
    T = MIX_TILE
    NB = T // BLOCK
    ktbufs = (kt0, kt1, kt2, kt3)
    vbufs = (v0, v1, v2, v3)

    @pl.when(t == 0)
    def _():
        state_ref[...] = jnp.zeros_like(state_ref)
        for buf in ktbufs:
            buf[:, 0:BLOCK] = jnp.zeros((KA, BLOCK), BF16)
        for buf in vbufs:
            buf[0:BLOCK, :] = jnp.zeros((BLOCK, KA), BF16)

    x = x_ref[...]
    xb = x.astype(BF16)

    za = _dot(xb, win_ref[:, C_QA:C_QR])
    q = (za[:, 0:QA] * (HEAD_DIM_A ** -0.5)).astype(BF16)
    for c in range(NB):
        for p in range(N_HEADS_A // 2):
            qa_s[(c * 4 + p) * BLOCK:(c * 4 + p + 1) * BLOCK, :] = (
                q[c * BLOCK:(c + 1) * BLOCK, p * KA:(p + 1) * KA])
    ka = za[:, QA:QA + KA]
    va = za[:, QA + KA:QA + 2 * KA]
    kwin_ref[...] = ka[T - WINDOW:, :]
    vwin_ref[...] = va[T - WINDOW:, :]
    ka_t = ka.T
    ka_tr = pltpu.roll(ka_t, HEAD_DIM_A, 0)
    top = lax.broadcasted_iota(jnp.int32, (KA, T), 0) < HEAD_DIM_A
    kt0[:, BLOCK:] = jnp.where(top, ka_t, 0.0).astype(BF16)
    kt1[:, BLOCK:] = jnp.where(top, 0.0, ka_tr).astype(BF16)
    kt2[:, BLOCK:] = jnp.where(top, ka_tr, 0.0).astype(BF16)
    kt3[:, BLOCK:] = jnp.where(top, 0.0, ka_t).astype(BF16)
    va_r = pltpu.roll(va, HEAD_DIM_A, 1)
    low = lax.broadcasted_iota(jnp.int32, (T, KA), 1) < HEAD_DIM_A
    v0[BLOCK:, :] = jnp.where(low, va, 0.0).astype(BF16)
    v1[BLOCK:, :] = jnp.where(low, 0.0, va_r).astype(BF16)
    v2[BLOCK:, :] = jnp.where(low, va_r, 0.0).astype(BF16)
    v3[BLOCK:, :] = jnp.where(low, 0.0, va).astype(BF16)

    row2 = lax.broadcasted_iota(jnp.int32, (2 * BLOCK, 1), 0) < BLOCK
    sink_col = []
    for g in range(N_KV_A):
        for e in range(2):
            sink_col.append(jnp.where(row2, sinks_ref[layer, 4 * g + e], sinks_ref[layer, 4 * g + 2 + e]))
    low_o = lax.broadcasted_iota(jnp.int32, (2 * BLOCK, KA), 1) < HEAD_DIM_A

    def scores(c):
        out = []
        for g in range(N_KV_A):
            qg = qa_s[(c * 4 + 2 * g) * BLOCK:(c * 4 + 2 * g + 2) * BLOCK, :]
            for e in range(2):
                out.append(_dot(qg, ktbufs[2 * g + e][:, c * BLOCK:(c + 2) * BLOCK]))
        return out

    def softmax(c, s_list):
        probs, invs = [], []
        for ge in range(4):
            if c == 0:
                bias = bias_ref[jnp.where(t == 0, 4, 0) + ge]
            else:
                bias = bias_ref[ge]
            s = s_list[ge] + bias
            m = jnp.maximum(jnp.max(s, axis=-1, keepdims=True), sink_col[ge])
            pe = jnp.exp(s - m)
            den = jnp.sum(pe, axis=-1, keepdims=True) + jnp.exp(sink_col[ge] - m)
            probs.append(pe.astype(BF16))
            invs.append(1.0 / den)
        return probs, invs

    def attend(c, probs, invs):
        for g in range(N_KV_A):
            o = (_dot(probs[2 * g], vbufs[2 * g][c * BLOCK:(c + 2) * BLOCK, :])
                 + _dot(probs[2 * g + 1], vbufs[2 * g + 1][c * BLOCK:(c + 2) * BLOCK, :]))
            o = (o * jnp.where(low_o, invs[2 * g], invs[2 * g + 1])).astype(BF16)
            oa_s[c * BLOCK:(c + 1) * BLOCK, 2 * g * KA:(2 * g + 1) * KA] = o[0:BLOCK]
            oa_s[c * BLOCK:(c + 1) * BLOCK, (2 * g + 1) * KA:(2 * g + 2) * KA] = o[BLOCK:]

    RC = RET_CHUNK
    NR = T // RC
    g_chunk = [math.exp(RC * lg) for lg in LOG_GAMMA]

    def ret_local(zr, vr, cc):
        sc, upd = [], []
        for h in range(N_HEADS_R):
            qh = zr[cc * RC:(cc + 1) * RC, DK_R * h:DK_R * (h + 1)].astype(BF16)
            kf = zr[cc * RC:(cc + 1) * RC, QR + DK_R * h:QR + DK_R * (h + 1)] * (DK_R ** -0.5)
            vh = vr[cc * RC:(cc + 1) * RC, DV_R * h:DV_R * (h + 1)]
            sc.append(_dot_nt(qh, kf.astype(BF16)))
            upd.append(_dot((kf * kdec_ref[h]).T.astype(BF16), vh))
        return sc, upd

    def ret_out(zr, vr, gr, cc, sc, s_in):
        for h in range(N_HEADS_R):
            qh = zr[cc * RC:(cc + 1) * RC, DK_R * h:DK_R * (h + 1)].astype(BF16)
            vh = vr[cc * RC:(cc + 1) * RC, DV_R * h:DV_R * (h + 1)]
            o = (_dot((sc[h] * dec_ref[h]).astype(BF16), vh)
                 + _dot(qh, s_in[h].astype(BF16)) * rowdec_ref[h])
            mu = jnp.mean(o, axis=-1, keepdims=True)
            d = o - mu
            var = jnp.mean(d * d, axis=-1, keepdims=True)
            on = d * lax.rsqrt(var + GN_EPS)
            gate = _silu(gr[cc * RC:(cc + 1) * RC, DV_R * h:DV_R * (h + 1)])
            orf_s[cc * RC:(cc + 1) * RC, DV_R * h:DV_R * (h + 1)] = (
                on * gn_ref[:, DV_R * h:DV_R * (h + 1)] * gate).astype(BF16)

    zr = _dot(xb, win_ref[:, C_QR:C_VR])
    s_cur = scores(0)
    vr = _dot(xb, win_ref[:, C_VR:C_GR]).astype(BF16)
    p_cur = softmax(0, s_cur)
    local = [ret_local(zr, vr, cc) for cc in range(NR)]
    s_nxt = scores(1)
    attend(0, *p_cur)
    gr = _dot(xb, win_ref[:, C_GR:C_GATE])
    p_cur = softmax(1, s_nxt)
    s_nxt = scores(2)
    attend(1, *p_cur)
    states = [[state_ref[h] for h in range(N_HEADS_R)]]
    for cc in range(NR):
        states.append([g_chunk[h] * states[cc][h] + local[cc][1][h] for h in range(N_HEADS_R)])
    for h in range(N_HEADS_R):
        state_ref[h] = states[NR][h]
    ret_out(zr, vr, gr, 0, local[0][0], states[0])
    gate_a = jax.nn.sigmoid(_dot(xb, win_ref[:, C_GATE:C_GATE + D_MODEL]))
    p_cur = softmax(2, s_nxt)
    s_nxt = scores(3)
    attend(2, *p_cur)
    for cc in range(1, NR):
        ret_out(zr, vr, gr, cc, local[cc][0], states[cc])
    gate_r = jax.nn.sigmoid(_dot(xb, win_ref[:, C_GATE + D_MODEL:IN_COLS]))
    p_cur = softmax(3, s_nxt)
    attend(3, *p_cur)

    for buf in ktbufs:
        buf[:, 0:BLOCK] = buf[:, T:T + BLOCK]
    for buf in vbufs:
        buf[0:BLOCK, :] = buf[T:T + BLOCK, :]

    merged = gate_a * _dot(oa_s[...], wa_ref[...]) + gate_r * _dot(orf_s[...], wr_ref[...])
    y = _dot(merged.astype(BF16), wo_ref[...])
    y_ref[...] = _layer_norm(ALPHA * x + y, g_ref[...], b_ref[...])


def _attention_bias():
    i = np.arange(BLOCK)[:, None]
    j = np.arange(2 * BLOCK)[None, :]
    dist = i + BLOCK - j
    band = (dist >= 0) & (dist <= WINDOW)
    out = np.zeros((8, 2 * BLOCK, 2 * BLOCK), np.float32)
    for first in range(2):
        valid = band & ((j >= BLOCK) | (first == 0))
        for g in range(N_KV_A):
            for e in range(2):
                for half in range(2):
                    h = 4 * g + 2 * half + e
                    out[4 * first + 2 * g + e, half * BLOCK:(half + 1) * BLOCK, :] = np.where(
                        valid, -SLOPES[h] * dist, NEG)
    return jnp.asarray(out)


def _retention_tables():
    pos = np.arange(RET_CHUNK, dtype=np.float64)
    diff = pos[:, None] - pos[None, :]
    lg = np.asarray(LOG_GAMMA)[:, None, None]
    dec = np.where(diff >= 0, np.exp(np.maximum(diff, 0.0) * lg), 0.0)
    rowdec = np.broadcast_to(np.exp((pos + 1.0)[None, :, None] * lg), (N_HEADS_R, RET_CHUNK, DV_R))
    kdec = np.broadcast_to(np.exp((RET_CHUNK - 1.0 - pos)[None, :, None] * lg),
                           (N_HEADS_R, RET_CHUNK, DK_R))
    return (jnp.asarray(dec, F32), jnp.asarray(rowdec, F32), jnp.asarray(kdec, F32))


def _mix_prompt(x, sinks, w_in, gn, wa, wr, wo, g, b, layer, side=()):
    B, S, _ = x.shape
    T = MIX_TILE
    nt = S // T
    bias = _attention_bias()
    dec, rowdec, kdec = _retention_tables()
    s_in, s_out, s_shape, s_args = _side_specs(side, lambda bi, ti: bi * nt + ti, B * nt)
    outs = pl.pallas_call(
        functools.partial(_mix_prompt_body, layer=layer, n_side=len(side)),
        grid=(B, nt),
        in_specs=[
            pl.BlockSpec(memory_space=pltpu.SMEM),
            pl.BlockSpec((None, T, D_MODEL), lambda bi, ti: (bi, ti, 0)),
            _whole(w_in),
            _whole(bias), _whole(dec), _whole(rowdec), _whole(kdec),
            _resident((1, VR), layer),
            _whole(wa),
            _whole(wr),
            _whole(wo),
            _resident((1, D_MODEL), layer),
            _resident((1, D_MODEL), layer),
        ] + s_in,
        out_specs=[
            pl.BlockSpec((None, T, D_MODEL), lambda bi, ti: (bi, ti, 0)),
            pl.BlockSpec((None, WINDOW, KA), lambda bi, ti: (bi, 0, 0)),
            pl.BlockSpec((None, WINDOW, KA), lambda bi, ti: (bi, 0, 0)),
            pl.BlockSpec((None, N_HEADS_R, DK_R, DV_R), lambda bi, ti: (bi, 0, 0, 0)),
        ] + s_out,
        out_shape=[
            jax.ShapeDtypeStruct((B, S, D_MODEL), F32),
            jax.ShapeDtypeStruct((B, WINDOW, KA), F32),
            jax.ShapeDtypeStruct((B, WINDOW, KA), F32),
            jax.ShapeDtypeStruct((B, N_HEADS_R, DK_R, DV_R), F32),
        ] + s_shape,
        scratch_shapes=(
            [pltpu.VMEM((T * 4, KA), BF16)]
            + [pltpu.VMEM((KA, T + BLOCK), BF16) for _ in range(4)]
            + [pltpu.VMEM((T + BLOCK, KA), BF16) for _ in range(4)]
            + [pltpu.VMEM((T, QA), BF16), pltpu.VMEM((T, VR), BF16)]),
        compiler_params=pltpu.CompilerParams(
            dimension_semantics=("arbitrary", "arbitrary"), vmem_limit_bytes=VMEM_LIMIT),
        name="mix_prompt",
    )(sinks, x, w_in, bias, dec, rowdec, kdec, gn, wa, wr, wo, g, b, *s_args)
    return outs[0], outs[1], outs[2], outs[3], outs[4:]


def _head_placement():
    e = np.zeros((QA, N_HEADS_A * KA), np.float32)
    for h in range(N_HEADS_A):
        g = h // GROUP_A
        for d in range(HEAD_DIM_A):
            e[h * HEAD_DIM_A + d, h * KA + g * HEAD_DIM_A + d] = 1.0
    return e


def _smp_proj_body(x_ref, win_ref, e_ref, z_ref, qm_ref):
    z = _dot(x_ref[...].astype(BF16), win_ref[...])
    z_ref[...] = z
    qa = (z[:, 0:QA] * (HEAD_DIM_A ** -0.5)).astype(BF16)
    qm_ref[...] = _dot(qa, e_ref[...])


def _smp_proj(x, w_in, e, layer):
    n = x.shape[0]
    return pl.pallas_call(
        _smp_proj_body,
        grid=(1,),
        in_specs=[
            pl.BlockSpec((n, D_MODEL), lambda i: (0, 0)),
            _whole(w_in),
            pl.BlockSpec((QA, N_HEADS_A * KA), lambda i: (0, 0)),
        ],
        out_specs=[
            pl.BlockSpec((n, IN_COLS), lambda i: (0, 0)),
            pl.BlockSpec((n, N_HEADS_A * KA), lambda i: (0, 0)),
        ],
        out_shape=[
            jax.ShapeDtypeStruct((n, IN_COLS), F32),
            jax.ShapeDtypeStruct((n, N_HEADS_A * KA), F32),
        ],
        compiler_params=pltpu.CompilerParams(
            dimension_semantics=("arbitrary",), vmem_limit_bytes=VMEM_LIMIT),
        name="smp_proj",
    )(x, w_in, e)


def _smp_core_body(sinks_ref, qm_ref, kvn_ref, ck_ref, cv_ref, zt_ref, vr_ref, st_ref, *rest, layer):
    o3_ref, or_ref, nk_ref, nv_ref, ns_ref = rest[-5:]
    row8 = lax.broadcasted_iota(jnp.int32, (N_HEADS_A, KA), 0)
    lane8 = lax.broadcasted_iota(jnp.int32, (N_HEADS_A, KA), 1)
    slope = jnp.zeros((N_HEADS_A, KA), F32)
    sinkv = jnp.zeros((N_HEADS_A, KA), F32)
    for h in range(N_HEADS_A):
        slope = jnp.where(row8 == h, SLOPES[h], slope)
        sinkv = jnp.where(row8 == h, sinks_ref[layer, h], sinkv)
    sink = sinkv[:, 0:1]
    bias = slope * (WINDOW - lane8).astype(F32)
    in_group = (lane8 < HEAD_DIM_A) == (row8 < GROUP_A)
    last_row = lax.broadcasted_iota(jnp.int32, (WINDOW, KA), 0) == WINDOW - 1
    zt = zt_ref[...]
    scores = [_dot_nt(qm_ref[bl].astype(BF16), ck_ref[bl].astype(BF16)) for bl in range(SMP_TILE)]
    probs = []
    for bl in range(SMP_TILE):
        qm = qm_ref[bl]
        kn = kvn_ref[bl:bl + 1, 0:KA]
        s = scores[bl] - bias
        s_new = jnp.sum(qm * kn, axis=-1, keepdims=True)
        m = jnp.maximum(jnp.maximum(jnp.max(s, axis=-1, keepdims=True), s_new), sink)
        p = jnp.exp(s - m)
        p_new = jnp.exp(s_new - m)
        inv = 1.0 / (jnp.sum(p, axis=-1, keepdims=True) + p_new + jnp.exp(sink - m))
        probs.append(((p * inv).astype(BF16), p_new * inv))
    for bl in range(SMP_TILE):
        kb = ck_ref[bl]
        vb = cv_ref[bl]
        kn = kvn_ref[bl:bl + 1, 0:KA]
        vn = kvn_ref[bl:bl + 1, KA:2 * KA]
        o = _dot(probs[bl][0], vb.astype(BF16)) + probs[bl][1] * vn
        o3_ref[bl] = jnp.where(in_group, o, 0.0)
        nk_ref[bl] = jnp.where(last_row, kn, pltpu.roll(kb, WINDOW - 1, 0))
        nv_ref[bl] = jnp.where(last_row, vn, pltpu.roll(vb, WINDOW - 1, 0))
    for bl in range(SMP_TILE):
        for h in range(N_HEADS_R):
            gamma = math.exp(LOG_GAMMA[h])
            qc = zt[DK_R * h:DK_R * (h + 1), bl:bl + 1]
            kc = zt[QR + DK_R * h:QR + DK_R * (h + 1), bl:bl + 1] * (DK_R ** -0.5)
            s0 = st_ref[bl, h]
            v = vr_ref[bl:bl + 1, DV_R * h:DV_R * (h + 1)]
            sc = jnp.sum(qc * kc, axis=0, keepdims=True)
            o_r = sc * v + jnp.sum(qc * s0, axis=0, keepdims=True) * gamma
            ns_ref[bl, h] = gamma * s0 + kc * v
            or_ref[bl:bl + 1, DV_R * h:DV_R * (h + 1)] = o_r


def _smp_core(sinks, qm3, kvn, ck, cv, zt, vr, st, stacked, layer):
    n = qm3.shape[0]
    tb = SMP_TILE
    in_specs = [
        pl.BlockSpec(memory_space=pltpu.SMEM),
        pl.BlockSpec((tb, N_HEADS_A, KA), lambda i: (i, 0, 0)),
        pl.BlockSpec((tb, 2 * KA), lambda i: (i, 0)),
        pl.BlockSpec((None, tb, WINDOW, KA), lambda i: (layer, i, 0, 0)),
        pl.BlockSpec((None, tb, WINDOW, KA), lambda i: (layer, i, 0, 0)),
        pl.BlockSpec((None, 2 * QR, tb), lambda i: (i, 0, 0)),
        pl.BlockSpec((tb, VR), lambda i: (i, 0)),
        pl.BlockSpec((None, tb, N_HEADS_R, DK_R, DV_R), lambda i: (layer, i, 0, 0, 0)),
    ]
    args = [sinks, qm3, kvn, ck, cv, zt, vr, st]
    aliases = {}
    if stacked is not None:
        for k, buf in enumerate(stacked):
            aliases[len(args)] = 2 + k
            in_specs.append(pl.BlockSpec(memory_space=pl.ANY))
            args.append(buf)
    return pl.pallas_call(
        functools.partial(_smp_core_body, layer=layer),
        grid=(n // tb,),
        in_specs=in_specs,
        out_specs=[
            pl.BlockSpec((tb, N_HEADS_A, KA), lambda i: (i, 0, 0)),
            pl.BlockSpec((tb, VR), lambda i: (i, 0)),
            pl.BlockSpec((None, tb, WINDOW, KA), lambda i: (layer, i, 0, 0)),
            pl.BlockSpec((None, tb, WINDOW, KA), lambda i: (layer, i, 0, 0)),
            pl.BlockSpec((None, tb, N_HEADS_R, DK_R, DV_R), lambda i: (layer, i, 0, 0, 0)),
        ],
        out_shape=[
            jax.ShapeDtypeStruct((n, N_HEADS_A, KA), F32),
            jax.ShapeDtypeStruct((n, VR), F32),
            jax.ShapeDtypeStruct((DEPTH, n, WINDOW, KA), F32),
            jax.ShapeDtypeStruct((DEPTH, n, WINDOW, KA), F32),
            jax.ShapeDtypeStruct((DEPTH, n, N_HEADS_R, DK_R, DV_R), F32),
        ],
        input_output_aliases=aliases,
        compiler_params=pltpu.CompilerParams(
            dimension_semantics=("arbitrary",), vmem_limit_bytes=VMEM_LIMIT),
        name="smp_core",
    )(*args)


def _smp_out_body(x_ref, zg_ref, o3_ref, or_ref, et_ref, gn_ref, wa_ref, wr_ref, wo_ref,
                  g_ref, b_ref, y_ref):
    x = x_ref[...]
    oa = _dot(o3_ref[...].astype(BF16), et_ref[...]).astype(BF16)
    gr = zg_ref[:, 0:VR]
    parts = []
    for h in range(N_HEADS_R):
        o = or_ref[:, DV_R * h:DV_R * (h + 1)]
        mu = jnp.mean(o, axis=-1, keepdims=True)
        d = o - mu
        var = jnp.mean(d * d, axis=-1, keepdims=True)
        parts.append(d * lax.rsqrt(var + GN_EPS))
    orn = jnp.concatenate(parts, axis=-1)
    orf = (orn * gn_ref[...] * _silu(gr)).astype(BF16)
    merged = (jax.nn.sigmoid(zg_ref[:, VR:VR + D_MODEL]) * _dot(oa, wa_ref[...])
              + jax.nn.sigmoid(zg_ref[:, VR + D_MODEL:]) * _dot(orf, wr_ref[...]))
    y = _dot(merged.astype(BF16), wo_ref[...])
    y_ref[...] = _layer_norm(ALPHA * x + y, g_ref[...], b_ref[...])


def _smp_out(x, zg, o3, o_r, et, gn, wa, wr, wo, g, b, layer):
    n = x.shape[0]
    whole = lambda a: pl.BlockSpec(a.shape, lambda i: (0,) * a.ndim)
    return pl.pallas_call(
        _smp_out_body,
        grid=(1,),
        in_specs=[
            whole(x), whole(zg), whole(o3), whole(o_r), whole(et),
            _resident((1, VR), layer),
            _whole(wa),
            _whole(wr),
            _whole(wo),
            _resident((1, D_MODEL), layer),
            _resident((1, D_MODEL), layer),
        ],
        out_specs=pl.BlockSpec((n, D_MODEL), lambda i: (0, 0)),
        out_shape=jax.ShapeDtypeStruct((n, D_MODEL), F32),
        compiler_params=pltpu.CompilerParams(
            dimension_semantics=("arbitrary",), vmem_limit_bytes=VMEM_LIMIT),
        name="smp_out",
    )(x, zg, o3, o_r, et, gn, wa, wr, wo, g, b)


def _mix_sample(x, sinks, w_in, cache_k, cache_v, state, stacked, gn, wa, wr, wo, g, b, e, et, layer):
    n = x.shape[0]
    z, qm = _smp_proj(x, w_in, e, layer)
    qm3 = qm.reshape(n, N_HEADS_A, KA)
    kvn = z[:, C_KA:C_QR]
    zt = z[:, C_QR:C_VR].T.reshape(2 * QR, n // SMP_TILE, SMP_TILE).transpose(1, 0, 2)
    vr = z[:, C_VR:C_GR]
    o3, o_r, nk, nv, ns = _smp_core(sinks, qm3, kvn, cache_k, cache_v, zt, vr, state, stacked, layer)
    y = _smp_out(x, z[:, C_GR:], o3.reshape(n, N_HEADS_A * KA), o_r, et, gn, wa, wr, wo, g, b, layer)
    return y, (nk, nv, ns)


def kernel(x_prompt, x_sample, cache_win_k, cache_win_v, state_ret, w_ff1_gu, w_ff1_dn, ln1_g, ln1_b, w_in, attn_sinks, ret_gn_g, w_br_a, w_br_r, w_o, ln2_g, ln2_b, w_ff2_gu, w_ff2_dn, ln3_g, ln3_b):
    B, S, _ = x_prompt.shape
    n = x_sample.shape[0]
    row = lambda p: p.reshape(DEPTH, 1, -1)
    g1, b1, g2, b2, g3, b3 = row(ln1_g), row(ln1_b), row(ln2_g), row(ln2_b), row(ln3_g), row(ln3_b)
    gn = row(ret_gn_g)
    e_np = _head_placement()
    e = jnp.asarray(e_np, BF16)
    et = jnp.asarray(e_np.T, BF16)
    ck = cache_win_k.reshape(DEPTH, n, WINDOW, KA)
    cv = cache_win_v.reshape(DEPTH, n, WINDOW, KA)

    xp = x_prompt.reshape(B * S, D_MODEL)
    xs = x_sample.reshape(n, D_MODEL)
    pk, pv, pr = [], [], []
    stacked = None
    ffn_steps = B * S // FFN_TILE
    mix_steps = B * S // MIX_TILE
    ffn1_w = (w_ff1_gu[0].astype(BF16), w_ff1_dn[0].astype(BF16))
    for l in range(DEPTH):
        xp, (win, wa, wr, wo) = _ffn(
            xp, *ffn1_w, g1, b1, l, FFN_TILE,
            side=[_side(w, l, ffn_steps) for w in (w_in, w_br_a, w_br_r, w_o)])
        xs, _ = _ffn(xs, *ffn1_w, g1, b1, l, n)
        yp, kw, vw, st, ffn2_w = _mix_prompt(
            xp.reshape(B, S, D_MODEL), attn_sinks, win, gn, wa, wr, wo, g2, b2, l,
            side=[_side(w, l, mix_steps) for w in (w_ff2_gu, w_ff2_dn)])
        xp = yp.reshape(B * S, D_MODEL)
        pk.append(kw)
        pv.append(vw)
        pr.append(st)
        xs, stacked = _mix_sample(xs, attn_sinks, win, ck, cv, state_ret, stacked, gn, wa, wr, wo,
                                  g2, b2, e, et, l)
        nxt = [_side(w, l + 1, ffn_steps) for w in (w_ff1_gu, w_ff1_dn)] if l + 1 < DEPTH else []
        xp, ffn1_next = _ffn(xp, *ffn2_w, g3, b3, l, FFN_TILE, side=nxt)
        xs, _ = _ffn(xs, *ffn2_w, g3, b3, l, n)
        ffn1_w = ffn1_next
    win_shape = (DEPTH, -1, WINDOW, N_KV_A, HEAD_DIM_A)
    return (xp.reshape(B, S, D_MODEL), xs.reshape(n, 1, D_MODEL),
            jnp.stack(pk).reshape(win_shape), jnp.stack(pv).reshape(win_shape), jnp.stack(pr),
            stacked[0].reshape(win_shape), stacked[1].reshape(win_shape), stacked[2])
```

```python
import functools
import math

import numpy as np
import jax
import jax.numpy as jnp
from jax import lax
from jax.experimental import pallas as pl
from jax.experimental.pallas import tpu as pltpu

D_MODEL = 1024
DEPTH = 4
N_HEADS_A = 8
N_KV_A = 2
HEAD_DIM_A = 64
GROUP_A = N_HEADS_A // N_KV_A
WINDOW = 128
BLOCK = 128
N_HEADS_R = 4
DK_R = 128
DV_R = 256
D_FF = 2816
ALPHA = (2.0 * DEPTH) ** 0.25
LN_EPS = 1e-5
GN_EPS = 1e-6
NEG = -1e30

QA = N_HEADS_A * HEAD_DIM_A
KA = N_KV_A * HEAD_DIM_A
QR = N_HEADS_R * DK_R
VR = N_HEADS_R * DV_R
C_QA, C_KA, C_VA = 0, QA, QA + KA
C_QR = QA + 2 * KA
C_KR = C_QR + QR
C_VR = C_KR + QR
C_GR = C_VR + VR
C_GATE = C_GR + VR
IN_COLS = C_GATE + 2 * D_MODEL

SLOPES = [2.0 ** (-8.0 * (h + 1) / N_HEADS_A) for h in range(N_HEADS_A)]
LOG_GAMMA = [math.log1p(-(2.0 ** (-5.0 - h))) for h in range(N_HEADS_R)]

BF16 = jnp.bfloat16
F32 = jnp.float32

FFN_CHUNK = 256
FFN_TILE = 1024
FFN_SUBTILE = 512
FFN_MIN_PIECE = 256
MIX_TILE = 512
RET_CHUNK = 256
SMP_TILE = 8
VMEM_LIMIT = 60 * 1024 * 1024


def _dot(a, b):
    return jnp.dot(a, b, preferred_element_type=F32)


def _dot_nt(a, b):
    return lax.dot_general(a, b, (((1,), (1,)), ((), ())), preferred_element_type=F32)


def _layer_norm(v, g, b):
    mu = jnp.mean(v, axis=-1, keepdims=True)
    d = v - mu
    var = jnp.mean(d * d, axis=-1, keepdims=True)
    return d * lax.rsqrt(var + LN_EPS) * g + b


def _silu(a):
    return a * jax.nn.sigmoid(a)


def _cast_side(refs, n_side):
    for k in range(n_side):
        refs[n_side + k][...] = refs[k][...].astype(BF16)


BF16_SUBLANES = 16


def _side(w, layer, total):
    r = w.shape[1]
    nblk = total
    while r % nblk or (r // nblk) % BF16_SUBLANES:
        nblk //= 2
    return (w, layer, r // nblk)


def _side_specs(side, step, total):
    in_specs, out_specs, out_shape, args = [], [], [], []
    for w, layer, rows in side:
        _, r, c = w.shape
        nblk = r // rows
        assert r % rows == 0 and total % nblk == 0
        idx = lambda *g, nblk=nblk: step(*g) // (total // nblk)
        in_specs.append(pl.BlockSpec((None, rows, c), lambda *g, idx=idx, layer=layer: (layer, idx(*g), 0)))
        out_specs.append(pl.BlockSpec((rows, c), lambda *g, idx=idx: (idx(*g), 0)))
        out_shape.append(jax.ShapeDtypeStruct((r, c), BF16))
        args.append(w)
    return in_specs, out_specs, out_shape, args


def _down_pieces(tm):
    pieces, left = [], tm
    while left > 2 * FFN_MIN_PIECE:
        pieces.append(left // 2)
        left -= left // 2
    return pieces + ([left // 2, left // 2] if left > FFN_MIN_PIECE else [left])


def _ffn_body(x_ref, xs_ref, wgu_ref, wdn_ref, g_ref, b_ref, *rest, n_side):
    side_in, rest = rest[:n_side], rest[n_side:]
    (o_ref, os_ref), rest = rest[:2], rest[2:]
    side_out, (h_ref,) = rest[:n_side], rest[n_side:]
    _cast_side(side_in + side_out, n_side)

    @pl.when(pl.program_id(0) == 0)
    def _():
        _ffn_rows(xs_ref, os_ref, h_ref, wgu_ref, wdn_ref, g_ref, b_ref)

    _ffn_rows(x_ref, o_ref, h_ref, wgu_ref, wdn_ref, g_ref, b_ref)


def _ffn_rows(x_ref, o_ref, h_ref, wgu_ref, wdn_ref, g_ref, b_ref):
    tm = x_ref.shape[0]
    sub = min(FFN_SUBTILE, tm)
    for s in range(tm // sub):
        rows = slice(s * sub, (s + 1) * sub)
        xb = x_ref[rows, :].astype(BF16)
        for j in range(D_FF // FFN_CHUNK):
            lo = j * FFN_CHUNK
            a = _dot(xb, wgu_ref[:, lo:lo + FFN_CHUNK])
            u = _dot(xb, wgu_ref[:, D_FF + lo:D_FF + lo + FFN_CHUNK])
            h_ref[rows, lo:lo + FFN_CHUNK] = (_silu(a) * u).astype(BF16)
    start = 0
    for size in _down_pieces(tm):
        rows = slice(start, start + size)
        d = _dot(h_ref[rows, :], wdn_ref[...])
        o_ref[rows, :] = _layer_norm(ALPHA * x_ref[rows, :] + 0.5 * d, g_ref[...], b_ref[...])
        start += size


def _resident(shape, layer):
    nd = len(shape)
    return pl.BlockSpec((None,) + shape, lambda *_: (layer,) + (0,) * nd,
                        pipeline_mode=pl.Buffered(1))


def _whole(a):
    return pl.BlockSpec(a.shape, lambda *_: (0,) * a.ndim, pipeline_mode=pl.Buffered(1))


def _ffn(x, xs, wgu, wdn, g, b, layer, tm, side=()):
    m = x.shape[0]
    n = xs.shape[0]
    steps = m // tm
    s_in, s_out, s_shape, s_args = _side_specs(side, lambda i: i, steps)
    outs = pl.pallas_call(
        functools.partial(_ffn_body, n_side=len(side)),
        grid=(steps,),
        in_specs=[
            pl.BlockSpec((tm, D_MODEL), lambda i: (i, 0)),
            pl.BlockSpec((n, D_MODEL), lambda i: (0, 0)),
            _whole(wgu),
            _whole(wdn),
            _resident((1, D_MODEL), layer),
            _resident((1, D_MODEL), layer),
        ] + s_in,
        out_specs=[pl.BlockSpec((tm, D_MODEL), lambda i: (i, 0)),
                   pl.BlockSpec((n, D_MODEL), lambda i: (0, 0))] + s_out,
        out_shape=[jax.ShapeDtypeStruct((m, D_MODEL), F32),
                   jax.ShapeDtypeStruct((n, D_MODEL), F32)] + s_shape,
        scratch_shapes=[pltpu.VMEM((tm, D_FF), BF16)],
        compiler_params=pltpu.CompilerParams(
            dimension_semantics=("arbitrary",), vmem_limit_bytes=VMEM_LIMIT),
        name="ffn",
    )(x, xs, wgu, wdn, g, b, *s_args)
    return outs[0], outs[1], outs[2:]


def _mix_prompt_body(sinks_ref, x_ref, win_ref, bias_ref, dec_ref, rowdec_ref, kdec_ref, gn_ref,
                     wa_ref, wr_ref, wo_ref, g_ref, b_ref, *rest, layer, n_side):
    side_in, rest = rest[:n_side], rest[n_side:]
    (y_ref, kwin_ref, vwin_ref, state_ref), rest = rest[:4], rest[4:]
    side_out, rest = rest[:n_side], rest[n_side:]
    qa_s, kt0, kt1, kt2, kt3, v0, v1, v2, v3, oa_s, orf_s, p_s, ga_s, gr_s = rest
    _mix_prompt_tile(sinks_ref, x_ref, win_ref, bias_ref, dec_ref, rowdec_ref, kdec_ref, gn_ref,
                     wa_ref, wr_ref, wo_ref, g_ref, b_ref, y_ref, kwin_ref, vwin_ref, state_ref,
                     qa_s, kt0, kt1, kt2, kt3, v0, v1, v2, v3, oa_s, orf_s, p_s, ga_s, gr_s, layer=layer)
    _cast_side(side_in + side_out, n_side)


def _mix_prompt_tile(sinks_ref, x_ref, win_ref, bias_ref, dec_ref, rowdec_ref, kdec_ref, gn_ref,
                     wa_ref, wr_ref, wo_ref, g_ref, b_ref,
                     y_ref, kwin_ref, vwin_ref, state_ref,
                     qa_s, kt0, kt1, kt2, kt3, v0, v1, v2, v3, oa_s, orf_s, p_s, ga_s, gr_s,
                     *, layer):
    t = pl.program_id(1)
    T = MIX_TILE
    NB = T // BLOCK
    ktbufs = (kt0, kt1, kt2, kt3)
    vbufs = (v0, v1, v2, v3)

    @pl.when(t == 0)
    def _():
        state_ref[...] = jnp.zeros_like(state_ref)
        for buf in ktbufs:
            buf[:, 0:BLOCK] = jnp.zeros((KA, BLOCK), BF16)
        for buf in vbufs:
            buf[0:BLOCK, :] = jnp.zeros((BLOCK, KA), BF16)

    x = x_ref[...]
    xb = x.astype(BF16)

    za = _dot(xb, win_ref[:, C_QA:C_QR])
    q = (za[:, 0:QA] * (HEAD_DIM_A ** -0.5)).astype(BF16)
    for c in range(NB):
        for p in range(N_HEADS_A // 2):
            qa_s[(c * 4 + p) * BLOCK:(c * 4 + p + 1) * BLOCK, :] = (
                q[c * BLOCK:(c + 1) * BLOCK, p * KA:(p + 1) * KA])
    ka = za[:, QA:QA + KA]
    va = za[:, QA + KA:QA + 2 * KA]
    kwin_ref[...] = ka[T - WINDOW:, :]
    vwin_ref[...] = va[T - WINDOW:, :]
    ka_t = ka.T
    ka_tr = pltpu.roll(ka_t, HEAD_DIM_A, 0)
    top = lax.broadcasted_iota(jnp.int32, (KA, T), 0) < HEAD_DIM_A
    kt0[:, BLOCK:] = jnp.where(top, ka_t, 0.0).astype(BF16)
    kt1[:, BLOCK:] = jnp.where(top, 0.0, ka_tr).astype(BF16)
    kt2[:, BLOCK:] = jnp.where(top, ka_tr, 0.0).astype(BF16)
    kt3[:, BLOCK:] = jnp.where(top, 0.0, ka_t).astype(BF16)
    va_r = pltpu.roll(va, HEAD_DIM_A, 1)
    low = lax.broadcasted_iota(jnp.int32, (T, KA), 1) < HEAD_DIM_A
    v0[BLOCK:, :] = jnp.where(low, va, 0.0).astype(BF16)
    v1[BLOCK:, :] = jnp.where(low, 0.0, va_r).astype(BF16)
    v2[BLOCK:, :] = jnp.where(low, va_r, 0.0).astype(BF16)
    v3[BLOCK:, :] = jnp.where(low, 0.0, va).astype(BF16)

    row2 = lax.broadcasted_iota(jnp.int32, (2 * BLOCK, 1), 0) < BLOCK
    sink_col = []
    for g in range(N_KV_A):
        for e in range(2):
            sink_col.append(jnp.where(row2, sinks_ref[layer, 4 * g + e], sinks_ref[layer, 4 * g + 2 + e]))

    def scores(c):
        out = []
        for g in range(N_KV_A):
            qg = qa_s[(c * 4 + 2 * g) * BLOCK:(c * 4 + 2 * g + 2) * BLOCK, :]
            for e in range(2):
                out.append(_dot(qg, ktbufs[2 * g + e][:, c * BLOCK:(c + 2) * BLOCK]))
        return out

    def softmax(c, s_list):
        for ge in range(4):
            if c == 0:
                bias = bias_ref[jnp.where(t == 0, 4, 0) + ge]
            else:
                bias = bias_ref[ge]
            s = s_list[ge] + bias
            m = jnp.maximum(jnp.max(s, axis=-1, keepdims=True), sink_col[ge])
            pe = jnp.exp(s - m)
            den = jnp.sum(pe, axis=-1, keepdims=True) + jnp.exp(sink_col[ge] - m)
            p_s[c * 4 + ge] = (pe * (1.0 / den)).astype(BF16)

    def attend(c):
        for g in range(N_KV_A):
            o = (_dot(p_s[c * 4 + 2 * g], vbufs[2 * g][c * BLOCK:(c + 2) * BLOCK, :])
                 + _dot(p_s[c * 4 + 2 * g + 1], vbufs[2 * g + 1][c * BLOCK:(c + 2) * BLOCK, :])).astype(BF16)
            oa_s[c * BLOCK:(c + 1) * BLOCK, 2 * g * KA:(2 * g + 1) * KA] = o[0:BLOCK]
            oa_s[c * BLOCK:(c + 1) * BLOCK, (2 * g + 1) * KA:(2 * g + 2) * KA] = o[BLOCK:]

    RC = RET_CHUNK
    NR = T // RC
    g_chunk = [math.exp(RC * lg) for lg in LOG_GAMMA]

    def ret_local(zr, vr, cc):
        sc, upd = [], []
        for h in range(N_HEADS_R):
            qh = zr[cc * RC:(cc + 1) * RC, DK_R * h:DK_R * (h + 1)].astype(BF16)
            kf = zr[cc * RC:(cc + 1) * RC, QR + DK_R * h:QR + DK_R * (h + 1)] * (DK_R ** -0.5)
            vh = vr[cc * RC:(cc + 1) * RC, DV_R * h:DV_R * (h + 1)]
            sc.append(_dot_nt(qh, kf.astype(BF16)))
            upd.append(_dot((kf * kdec_ref[h]).T.astype(BF16), vh))
        return sc, upd

    def ret_out(zr, vr, gr, cc, sc, s_in):
        for h in range(N_HEADS_R):
            qh = zr[cc * RC:(cc + 1) * RC, DK_R * h:DK_R * (h + 1)].astype(BF16)
            vh = vr[cc * RC:(cc + 1) * RC, DV_R * h:DV_R * (h + 1)]
            o = (_dot((sc[h] * dec_ref[h]).astype(BF16), vh)
                 + _dot(qh, s_in[h].astype(BF16)) * rowdec_ref[h])
            mu = jnp.mean(o, axis=-1, keepdims=True)
            d = o - mu
            var = jnp.mean(d * d, axis=-1, keepdims=True)
            on = d * lax.rsqrt(var + GN_EPS)
            gate = _silu(gr[cc * RC:(cc + 1) * RC, DV_R * h:DV_R * (h + 1)])
            orf_s[cc * RC:(cc + 1) * RC, DV_R * h:DV_R * (h + 1)] = (
                on * gn_ref[:, DV_R * h:DV_R * (h + 1)] * gate).astype(BF16)

    s_all = [scores(c) for c in range(NB)]
    ga_s[...] = jax.nn.sigmoid(_dot(xb, win_ref[:, C_GATE:C_GATE + D_MODEL]))
    for c in range(NB):
        softmax(c, s_all[c])
    gr_s[...] = jax.nn.sigmoid(_dot(xb, win_ref[:, C_GATE + D_MODEL:IN_COLS]))
    zr = _dot(xb, win_ref[:, C_QR:C_VR])
    vr = _dot(xb, win_ref[:, C_VR:C_GR]).astype(BF16)
    gr = _dot(xb, win_ref[:, C_GR:C_GATE])
    for c in range(NB):
        attend(c)
    for buf in ktbufs:
        buf[:, 0:BLOCK] = buf[:, T:T + BLOCK]
    for buf in vbufs:
        buf[0:BLOCK, :] = buf[T:T + BLOCK, :]

    local = [ret_local(zr, vr, cc) for cc in range(NR)]
    states = [[state_ref[h] for h in range(N_HEADS_R)]]
    for cc in range(NR):
        states.append([g_chunk[h] * states[cc][h] + local[cc][1][h] for h in range(N_HEADS_R)])
    for h in range(N_HEADS_R):
        state_ref[h] = states[NR][h]
    ya = _dot(oa_s[...], wa_ref[...])
    for cc in range(NR):
        rows = slice(cc * RC, (cc + 1) * RC)
        ret_out(zr, vr, gr, cc, local[cc][0], states[cc])
        merged = ga_s[rows, :] * ya[rows, :] + gr_s[rows, :] * _dot(orf_s[rows, :], wr_ref[...])
        y = _dot(merged.astype(BF16), wo_ref[...])
        y_ref[rows, :] = _layer_norm(ALPHA * x[rows, :] + y, g_ref[...], b_ref[...])


def _attention_bias():
    i = np.arange(BLOCK)[:, None]
    j = np.arange(2 * BLOCK)[None, :]
    dist = i + BLOCK - j
    band = (dist >= 0) & (dist <= WINDOW)
    out = np.zeros((8, 2 * BLOCK, 2 * BLOCK), np.float32)
    for first in range(2):
        valid = band & ((j >= BLOCK) | (first == 0))
        for g in range(N_KV_A):
            for e in range(2):
                for half in range(2):
                    h = 4 * g + 2 * half + e
                    out[4 * first + 2 * g + e, half * BLOCK:(half + 1) * BLOCK, :] = np.where(
                        valid, -SLOPES[h] * dist, NEG)
    return jnp.asarray(out)


def _retention_tables():
    pos = np.arange(RET_CHUNK, dtype=np.float64)
    diff = pos[:, None] - pos[None, :]
    lg = np.asarray(LOG_GAMMA)[:, None, None]
    dec = np.where(diff >= 0, np.exp(np.maximum(diff, 0.0) * lg), 0.0)
    rowdec = np.broadcast_to(np.exp((pos + 1.0)[None, :, None] * lg), (N_HEADS_R, RET_CHUNK, DV_R))
    kdec = np.broadcast_to(np.exp((RET_CHUNK - 1.0 - pos)[None, :, None] * lg),
                           (N_HEADS_R, RET_CHUNK, DK_R))
    return (jnp.asarray(dec, F32), jnp.asarray(rowdec, F32), jnp.asarray(kdec, F32))


def _mix_prompt(x, sinks, w_in, gn, wa, wr, wo, g, b, layer, side=()):
    B, S, _ = x.shape
    T = MIX_TILE
    nt = S // T
    bias = _attention_bias()
    dec, rowdec, kdec = _retention_tables()
    s_in, s_out, s_shape, s_args = _side_specs(side, lambda bi, ti: bi * nt + ti, B * nt)
    outs = pl.pallas_call(
        functools.partial(_mix_prompt_body, layer=layer, n_side=len(side)),
        grid=(B, nt),
        in_specs=[
            pl.BlockSpec(memory_space=pltpu.SMEM),
            pl.BlockSpec((None, T, D_MODEL), lambda bi, ti: (bi, ti, 0)),
            _whole(w_in),
            _whole(bias), _whole(dec), _whole(rowdec), _whole(kdec),
            _resident((1, VR), layer),
            _whole(wa),
            _whole(wr),
            _whole(wo),
            _resident((1, D_MODEL), layer),
            _resident((1, D_MODEL), layer),
        ] + s_in,
        out_specs=[
            pl.BlockSpec((None, T, D_MODEL), lambda bi, ti: (bi, ti, 0)),
            pl.BlockSpec((None, WINDOW, KA), lambda bi, ti: (bi, 0, 0)),
            pl.BlockSpec((None, WINDOW, KA), lambda bi, ti: (bi, 0, 0)),
            pl.BlockSpec((None, N_HEADS_R, DK_R, DV_R), lambda bi, ti: (bi, 0, 0, 0)),
        ] + s_out,
        out_shape=[
            jax.ShapeDtypeStruct((B, S, D_MODEL), F32),
            jax.ShapeDtypeStruct((B, WINDOW, KA), F32),
            jax.ShapeDtypeStruct((B, WINDOW, KA), F32),
            jax.ShapeDtypeStruct((B, N_HEADS_R, DK_R, DV_R), F32),
        ] + s_shape,
        scratch_shapes=(
            [pltpu.VMEM((T * 4, KA), BF16)]
            + [pltpu.VMEM((KA, T + BLOCK), BF16) for _ in range(4)]
            + [pltpu.VMEM((T + BLOCK, KA), BF16) for _ in range(4)]
            + [pltpu.VMEM((T, QA), BF16), pltpu.VMEM((T, VR), BF16),
               pltpu.VMEM((T // BLOCK * 4, 2 * BLOCK, 2 * BLOCK), BF16),
               pltpu.VMEM((T, D_MODEL), F32), pltpu.VMEM((T, D_MODEL), F32)]),
        compiler_params=pltpu.CompilerParams(
            dimension_semantics=("arbitrary", "arbitrary"), vmem_limit_bytes=VMEM_LIMIT),
        name="mix_prompt",
    )(sinks, x, w_in, bias, dec, rowdec, kdec, gn, wa, wr, wo, g, b, *s_args)
    return outs[0], outs[1], outs[2], outs[3], outs[4:]


def _head_placement():
    e = np.zeros((QA, N_HEADS_A * KA), np.float32)
    for h in range(N_HEADS_A):
        g = h // GROUP_A
        for d in range(HEAD_DIM_A):
            e[h * HEAD_DIM_A + d, h * KA + g * HEAD_DIM_A + d] = 1.0
    return e


def _smp_proj_body(x_ref, win_ref, e_ref, z_ref, qm_ref):
    z = _dot(x_ref[...].astype(BF16), win_ref[...])
    z_ref[...] = z
    qa = (z[:, 0:QA] * (HEAD_DIM_A ** -0.5)).astype(BF16)
    qm_ref[...] = _dot(qa, e_ref[...])


def _smp_proj(x, w_in, e, layer):
    n = x.shape[0]
    return pl.pallas_call(
        _smp_proj_body,
        grid=(1,),
        in_specs=[
            pl.BlockSpec((n, D_MODEL), lambda i: (0, 0)),
            _whole(w_in),
            pl.BlockSpec((QA, N_HEADS_A * KA), lambda i: (0, 0)),
        ],
        out_specs=[
            pl.BlockSpec((n, IN_COLS), lambda i: (0, 0)),
            pl.BlockSpec((n, N_HEADS_A * KA), lambda i: (0, 0)),
        ],
        out_shape=[
            jax.ShapeDtypeStruct((n, IN_COLS), F32),
            jax.ShapeDtypeStruct((n, N_HEADS_A * KA), F32),
        ],
        compiler_params=pltpu.CompilerParams(
            dimension_semantics=("arbitrary",), vmem_limit_bytes=VMEM_LIMIT),
        name="smp_proj",
    )(x, w_in, e)


def _smp_core_body(sinks_ref, qm_ref, kvn_ref, ck_ref, cv_ref, zt_ref, vr_ref, st_ref, *rest, layer):
    o3_ref, or_ref, nk_ref, nv_ref, ns_ref = rest[-5:]
    row8 = lax.broadcasted_iota(jnp.int32, (N_HEADS_A, KA), 0)
    lane8 = lax.broadcasted_iota(jnp.int32, (N_HEADS_A, KA), 1)
    slope = jnp.zeros((N_HEADS_A, KA), F32)
    sinkv = jnp.zeros((N_HEADS_A, KA), F32)
    for h in range(N_HEADS_A):
        slope = jnp.where(row8 == h, SLOPES[h], slope)
        sinkv = jnp.where(row8 == h, sinks_ref[layer, h], sinkv)
    sink = sinkv[:, 0:1]
    bias = slope * (WINDOW - lane8).astype(F32)
    in_group = (lane8 < HEAD_DIM_A) == (row8 < GROUP_A)
    last_row = lax.broadcasted_iota(jnp.int32, (WINDOW, KA), 0) == WINDOW - 1
    zt = zt_ref[...]
    scores = [_dot_nt(qm_ref[bl].astype(BF16), ck_ref[bl].astype(BF16)) for bl in range(SMP_TILE)]
    probs = []
    for bl in range(SMP_TILE):
        qm = qm_ref[bl]
        kn = kvn_ref[bl:bl + 1, 0:KA]
        s = scores[bl] - bias
        s_new = jnp.sum(qm * kn, axis=-1, keepdims=True)
        m = jnp.maximum(jnp.maximum(jnp.max(s, axis=-1, keepdims=True), s_new), sink)
        p = jnp.exp(s - m)
        p_new = jnp.exp(s_new - m)
        inv = 1.0 / (jnp.sum(p, axis=-1, keepdims=True) + p_new + jnp.exp(sink - m))
        probs.append(((p * inv).astype(BF16), p_new * inv))
    for bl in range(SMP_TILE):
        kb = ck_ref[bl]
        vb = cv_ref[bl]
        kn = kvn_ref[bl:bl + 1, 0:KA]
        vn = kvn_ref[bl:bl + 1, KA:2 * KA]
        o = _dot(probs[bl][0], vb.astype(BF16)) + probs[bl][1] * vn
        o3_ref[bl] = jnp.where(in_group, o, 0.0)
        nk_ref[bl] = jnp.where(last_row, kn, pltpu.roll(kb, WINDOW - 1, 0))
        nv_ref[bl] = jnp.where(last_row, vn, pltpu.roll(vb, WINDOW - 1, 0))
    for bl in range(SMP_TILE):
        for h in range(N_HEADS_R):
            gamma = math.exp(LOG_GAMMA[h])
            qc = zt[DK_R * h:DK_R * (h + 1), bl:bl + 1]
            kc = zt[QR + DK_R * h:QR + DK_R * (h + 1), bl:bl + 1] * (DK_R ** -0.5)
            s0 = st_ref[bl, h]
            v = vr_ref[bl:bl + 1, DV_R * h:DV_R * (h + 1)]
            sc = jnp.sum(qc * kc, axis=0, keepdims=True)
            o_r = sc * v + jnp.sum(qc * s0, axis=0, keepdims=True) * gamma
            ns_ref[bl, h] = gamma * s0 + kc * v
            or_ref[bl:bl + 1, DV_R * h:DV_R * (h + 1)] = o_r


def _smp_core(sinks, qm3, kvn, ck, cv, zt, vr, st, stacked, layer):
    n = qm3.shape[0]
    tb = SMP_TILE
    in_specs = [
        pl.BlockSpec(memory_space=pltpu.SMEM),
        pl.BlockSpec((tb, N_HEADS_A, KA), lambda i: (i, 0, 0)),
        pl.BlockSpec((tb, 2 * KA), lambda i: (i, 0)),
        pl.BlockSpec((None, tb, WINDOW, KA), lambda i: (layer, i, 0, 0)),
        pl.BlockSpec((None, tb, WINDOW, KA), lambda i: (layer, i, 0, 0)),
        pl.BlockSpec((None, 2 * QR, tb), lambda i: (i, 0, 0)),
        pl.BlockSpec((tb, VR), lambda i: (i, 0)),
        pl.BlockSpec((None, tb, N_HEADS_R, DK_R, DV_R), lambda i: (layer, i, 0, 0, 0)),
    ]
    args = [sinks, qm3, kvn, ck, cv, zt, vr, st]
    aliases = {}
    if stacked is not None:
        for k, buf in enumerate(stacked):
            aliases[len(args)] = 2 + k
            in_specs.append(pl.BlockSpec(memory_space=pl.ANY))
            args.append(buf)
    return pl.pallas_call(
        functools.partial(_smp_core_body, layer=layer),
        grid=(n // tb,),
        in_specs=in_specs,
        out_specs=[
            pl.BlockSpec((tb, N_HEADS_A, KA), lambda i: (i, 0, 0)),
            pl.BlockSpec((tb, VR), lambda i: (i, 0)),
            pl.BlockSpec((None, tb, WINDOW, KA), lambda i: (layer, i, 0, 0)),
            pl.BlockSpec((None, tb, WINDOW, KA), lambda i: (layer, i, 0, 0)),
            pl.BlockSpec((None, tb, N_HEADS_R, DK_R, DV_R), lambda i: (layer, i, 0, 0, 0)),
        ],
        out_shape=[
            jax.ShapeDtypeStruct((n, N_HEADS_A, KA), F32),
            jax.ShapeDtypeStruct((n, VR), F32),
            jax.ShapeDtypeStruct((DEPTH, n, WINDOW, KA), F32),
            jax.ShapeDtypeStruct((DEPTH, n, WINDOW, KA), F32),
            jax.ShapeDtypeStruct((DEPTH, n, N_HEADS_R, DK_R, DV_R), F32),
        ],
        input_output_aliases=aliases,
        compiler_params=pltpu.CompilerParams(
            dimension_semantics=("arbitrary",), vmem_limit_bytes=VMEM_LIMIT),
        name="smp_core",
    )(*args)


def _smp_out_body(x_ref, zg_ref, o3_ref, or_ref, et_ref, gn_ref, wa_ref, wr_ref, wo_ref,
                  g_ref, b_ref, y_ref):
    x = x_ref[...]
    oa = _dot(o3_ref[...].astype(BF16), et_ref[...]).astype(BF16)
    gr = zg_ref[:, 0:VR]
    parts = []
    for h in range(N_HEADS_R):
        o = or_ref[:, DV_R * h:DV_R * (h + 1)]
        mu = jnp.mean(o, axis=-1, keepdims=True)
        d = o - mu
        var = jnp.mean(d * d, axis=-1, keepdims=True)
        parts.append(d * lax.rsqrt(var + GN_EPS))
    orn = jnp.concatenate(parts, axis=-1)
    orf = (orn * gn_ref[...] * _silu(gr)).astype(BF16)
    merged = (jax.nn.sigmoid(zg_ref[:, VR:VR + D_MODEL]) * _dot(oa, wa_ref[...])
              + jax.nn.sigmoid(zg_ref[:, VR + D_MODEL:]) * _dot(orf, wr_ref[...]))
    y = _dot(merged.astype(BF16), wo_ref[...])
    y_ref[...] = _layer_norm(ALPHA * x + y, g_ref[...], b_ref[...])


def _smp_out(x, zg, o3, o_r, et, gn, wa, wr, wo, g, b, layer):
    n = x.shape[0]
    whole = lambda a: pl.BlockSpec(a.shape, lambda i: (0,) * a.ndim)
    return pl.pallas_call(
        _smp_out_body,
        grid=(1,),
        in_specs=[
            whole(x), whole(zg), whole(o3), whole(o_r), whole(et),
            _resident((1, VR), layer),
            _whole(wa),
            _whole(wr),
            _whole(wo),
            _resident((1, D_MODEL), layer),
            _resident((1, D_MODEL), layer),
        ],
        out_specs=pl.BlockSpec((n, D_MODEL), lambda i: (0, 0)),
        out_shape=jax.ShapeDtypeStruct((n, D_MODEL), F32),
        compiler_params=pltpu.CompilerParams(
            dimension_semantics=("arbitrary",), vmem_limit_bytes=VMEM_LIMIT),
        name="smp_out",
    )(x, zg, o3, o_r, et, gn, wa, wr, wo, g, b)


def _mix_sample(x, sinks, w_in, cache_k, cache_v, state, stacked, gn, wa, wr, wo, g, b, e, et, layer):
    n = x.shape[0]
    z, qm = _smp_proj(x, w_in, e, layer)
    qm3 = qm.reshape(n, N_HEADS_A, KA)
    kvn = z[:, C_KA:C_QR]
    zt = z[:, C_QR:C_VR].T.reshape(2 * QR, n // SMP_TILE, SMP_TILE).transpose(1, 0, 2)
    vr = z[:, C_VR:C_GR]
    o3, o_r, nk, nv, ns = _smp_core(sinks, qm3, kvn, cache_k, cache_v, zt, vr, state, stacked, layer)
    y = _smp_out(x, z[:, C_GR:], o3.reshape(n, N_HEADS_A * KA), o_r, et, gn, wa, wr, wo, g, b, layer)
    return y, (nk, nv, ns)


def kernel(x_prompt, x_sample, cache_win_k, cache_win_v, state_ret, w_ff1_gu, w_ff1_dn, ln1_g, ln1_b, w_in, attn_sinks, ret_gn_g, w_br_a, w_br_r, w_o, ln2_g, ln2_b, w_ff2_gu, w_ff2_dn, ln3_g, ln3_b):
    B, S, _ = x_prompt.shape
    n = x_sample.shape[0]
    row = lambda p: p.reshape(DEPTH, 1, -1)
    g1, b1, g2, b2, g3, b3 = row(ln1_g), row(ln1_b), row(ln2_g), row(ln2_b), row(ln3_g), row(ln3_b)
    gn = row(ret_gn_g)
    e_np = _head_placement()
    e = jnp.asarray(e_np, BF16)
    et = jnp.asarray(e_np.T, BF16)
    ck = cache_win_k.reshape(DEPTH, n, WINDOW, KA)
    cv = cache_win_v.reshape(DEPTH, n, WINDOW, KA)

    xp = x_prompt.reshape(B * S, D_MODEL)
    xs = x_sample.reshape(n, D_MODEL)
    pk, pv, pr = [], [], []
    stacked = None
    ffn_steps = B * S // FFN_TILE
    mix_steps = B * S // MIX_TILE
    ffn1_w = (w_ff1_gu[0].astype(BF16), w_ff1_dn[0].astype(BF16))
    for l in range(DEPTH):
        xp, xs, (win, wa, wr, wo) = _ffn(
            xp, xs, *ffn1_w, g1, b1, l, FFN_TILE,
            side=[_side(w, l, ffn_steps) for w in (w_in, w_br_a, w_br_r, w_o)])
        yp, kw, vw, st, ffn2_w = _mix_prompt(
            xp.reshape(B, S, D_MODEL), attn_sinks, win, gn, wa, wr, wo, g2, b2, l,
            side=[_side(w, l, mix_steps) for w in (w_ff2_gu, w_ff2_dn)])
        xp = yp.reshape(B * S, D_MODEL)
        pk.append(kw)
        pv.append(vw)
        pr.append(st)
        xs, stacked = _mix_sample(xs, attn_sinks, win, ck, cv, state_ret, stacked, gn, wa, wr, wo,
                                  g2, b2, e, et, l)
        nxt = [_side(w, l + 1, ffn_steps) for w in (w_ff1_gu, w_ff1_dn)] if l + 1 < DEPTH else []
        xp, xs, ffn1_w = _ffn(xp, xs, *ffn2_w, g3, b3, l, FFN_TILE, side=nxt)
    win_shape = (DEPTH, -1, WINDOW, N_KV_A, HEAD_DIM_A)
    return (xp.reshape(B, S, D_MODEL), xs.reshape(n, 1, D_MODEL),
            jnp.stack(pk).reshape(win_shape), jnp.stack(pv).reshape(win_shape), jnp.stack(pr),
            stacked[0].reshape(win_shape), stacked[1].reshape(win_shape), stacked[2])
```

```python
import functools
import math

import numpy as np
import jax
import jax.numpy as jnp
from jax import lax
from jax.experimental import pallas as pl
from jax.experimental.pallas import tpu as pltpu

D_MODEL = 1024
DEPTH = 4
N_HEADS_A = 8
N_KV_A = 2
HEAD_DIM_A = 64
GROUP_A = N_HEADS_A // N_KV_A
WINDOW = 128
BLOCK = 128
N_HEADS_R = 4
DK_R = 128
DV_R = 256
D_FF = 2816
ALPHA = (2.0 * DEPTH) ** 0.25
LN_EPS = 1e-5
GN_EPS = 1e-6
NEG = -1e30

QA = N_HEADS_A * HEAD_DIM_A
KA = N_KV_A * HEAD_DIM_A
QR = N_HEADS_R * DK_R
VR = N_HEADS_R * DV_R
C_QA, C_KA, C_VA = 0, QA, QA + KA
C_QR = QA + 2 * KA
C_KR = C_QR + QR
C_VR = C_KR + QR
C_GR = C_VR + VR
C_GATE = C_GR + VR
IN_COLS = C_GATE + 2 * D_MODEL

SLOPES = [2.0 ** (-8.0 * (h + 1) / N_HEADS_A) for h in range(N_HEADS_A)]
LOG_GAMMA = [math.log1p(-(2.0 ** (-5.0 - h))) for h in range(N_HEADS_R)]

BF16 = jnp.bfloat16
F32 = jnp.float32

FFN_CHUNK = 256
FFN_TILE = 1024
FFN_SUBTILE = 512
FFN_MIN_PIECE = 256
MIX_TILE = 512
RET_CHUNK = 256
SMP_TILE = 8
VMEM_LIMIT = 60 * 1024 * 1024


def _dot(a, b):
    return jnp.dot(a, b, preferred_element_type=F32)


def _dot_nt(a, b):
    return lax.dot_general(a, b, (((1,), (1,)), ((), ())), preferred_element_type=F32)


def _layer_norm(v, g, b):
    mu = jnp.mean(v, axis=-1, keepdims=True)
    d = v - mu
    var = jnp.mean(d * d, axis=-1, keepdims=True)
    return d * lax.rsqrt(var + LN_EPS) * g + b


def _silu(a):
    return a * jax.nn.sigmoid(a)


def _cast_side(refs, n_side):
    for k in range(n_side):
        refs[n_side + k][...] = refs[k][...].astype(BF16)


BF16_SUBLANES = 16


def _side(w, layer, total):
    r = w.shape[1]
    nblk = total
    while r % nblk or (r // nblk) % BF16_SUBLANES:
        nblk //= 2
    return (w, layer, r // nblk)


def _side_specs(side, step, total):
    in_specs, out_specs, out_shape, args = [], [], [], []
    for w, layer, rows in side:
        _, r, c = w.shape
        nblk = r // rows
        assert r % rows == 0 and total % nblk == 0
        idx = lambda *g, nblk=nblk: step(*g) // (total // nblk)
        in_specs.append(pl.BlockSpec((None, rows, c), lambda *g, idx=idx, layer=layer: (layer, idx(*g), 0)))
        out_specs.append(pl.BlockSpec((rows, c), lambda *g, idx=idx: (idx(*g), 0)))
        out_shape.append(jax.ShapeDtypeStruct((r, c), BF16))
        args.append(w)
    return in_specs, out_specs, out_shape, args


def _down_pieces(tm):
    pieces, left = [], tm
    while left > 2 * FFN_MIN_PIECE:
        pieces.append(left // 2)
        left -= left // 2
    return pieces + ([left // 2, left // 2] if left > FFN_MIN_PIECE else [left])


def _ffn_body(x_ref, xs_ref, wgu_ref, wdn_ref, g_ref, b_ref, *rest, n_side):
    side_in, rest = rest[:n_side], rest[n_side:]
    (o_ref, os_ref), rest = rest[:2], rest[2:]
    side_out, (h_ref,) = rest[:n_side], rest[n_side:]
    _cast_side(side_in + side_out, n_side)

    @pl.when(pl.program_id(0) == 0)
    def _():
        _ffn_rows(xs_ref, os_ref, h_ref, wgu_ref, wdn_ref, g_ref, b_ref)

    _ffn_rows(x_ref, o_ref, h_ref, wgu_ref, wdn_ref, g_ref, b_ref)


def _ffn_rows(x_ref, o_ref, h_ref, wgu_ref, wdn_ref, g_ref, b_ref):
    tm = x_ref.shape[0]
    sub = min(FFN_SUBTILE, tm)
    for s in range(tm // sub):
        rows = slice(s * sub, (s + 1) * sub)
        xb = x_ref[rows, :].astype(BF16)
        for j in range(D_FF // FFN_CHUNK):
            lo = j * FFN_CHUNK
            a = _dot(xb, wgu_ref[:, lo:lo + FFN_CHUNK])
            u = _dot(xb, wgu_ref[:, D_FF + lo:D_FF + lo + FFN_CHUNK])
            h_ref[rows, lo:lo + FFN_CHUNK] = (_silu(a) * u).astype(BF16)
    start = 0
    for size in _down_pieces(tm):
        rows = slice(start, start + size)
        d = _dot(h_ref[rows, :], wdn_ref[...])
        o_ref[rows, :] = _layer_norm(ALPHA * x_ref[rows, :] + 0.5 * d, g_ref[...], b_ref[...])
        start += size


def _resident(shape, layer):
    nd = len(shape)
    return pl.BlockSpec((None,) + shape, lambda *_: (layer,) + (0,) * nd,
                        pipeline_mode=pl.Buffered(1))


def _whole(a):
    return pl.BlockSpec(a.shape, lambda *_: (0,) * a.ndim, pipeline_mode=pl.Buffered(1))


def _ffn(x, xs, wgu, wdn, g, b, layer, tm, side=()):
    m = x.shape[0]
    n = xs.shape[0]
    steps = m // tm
    s_in, s_out, s_shape, s_args = _side_specs(side, lambda i: i, steps)
    outs = pl.pallas_call(
        functools.partial(_ffn_body, n_side=len(side)),
        grid=(steps,),
        in_specs=[
            pl.BlockSpec((tm, D_MODEL), lambda i: (i, 0)),
            pl.BlockSpec((n, D_MODEL), lambda i: (0, 0)),
            _whole(wgu),
            _whole(wdn),
            _resident((1, D_MODEL), layer),
            _resident((1, D_MODEL), layer),
        ] + s_in,
        out_specs=[pl.BlockSpec((tm, D_MODEL), lambda i: (i, 0)),
                   pl.BlockSpec((n, D_MODEL), lambda i: (0, 0))] + s_out,
        out_shape=[jax.ShapeDtypeStruct((m, D_MODEL), F32),
                   jax.ShapeDtypeStruct((n, D_MODEL), F32)] + s_shape,
        scratch_shapes=[pltpu.VMEM((tm, D_FF), BF16)],
        compiler_params=pltpu.CompilerParams(
            dimension_semantics=("arbitrary",), vmem_limit_bytes=VMEM_LIMIT),
        name="ffn",
    )(x, xs, wgu, wdn, g, b, *s_args)
    return outs[0], outs[1], outs[2:]


def _mix_prompt_body(sinks_ref, x_ref, win_ref, bias_ref, dec_ref, rowdec_ref, kdec_ref, gn_ref,
                     wa_ref, wr_ref, wo_ref, g_ref, b_ref, *rest, layer, n_side):
    side_in, rest = rest[:n_side], rest[n_side:]
    (y_ref, kwin_ref, vwin_ref, state_ref), rest = rest[:4], rest[4:]
    side_out, rest = rest[:n_side], rest[n_side:]
    qt_s, vt0, vt1, vt2, vt3, k0, k1, k2, k3, oa_s, orf_s, p_s, inv_s, ga_s, gr_s = rest
    _mix_prompt_tile(sinks_ref, x_ref, win_ref, bias_ref, dec_ref, rowdec_ref, kdec_ref, gn_ref,
                     wa_ref, wr_ref, wo_ref, g_ref, b_ref, y_ref, kwin_ref, vwin_ref, state_ref,
                     qt_s, vt0, vt1, vt2, vt3, k0, k1, k2, k3, oa_s, orf_s, p_s, inv_s, ga_s, gr_s, layer=layer)
    _cast_side(side_in + side_out, n_side)


def _mix_prompt_tile(sinks_ref, x_ref, win_ref, bias_ref, dec_ref, rowdec_ref, kdec_ref, gn_ref,
                     wa_ref, wr_ref, wo_ref, g_ref, b_ref,
                     y_ref, kwin_ref, vwin_ref, state_ref,
                     qt_s, vt0, vt1, vt2, vt3, k0, k1, k2, k3, oa_s, orf_s, p_s, inv_s, ga_s, gr_s,
                     *, layer):
    t = pl.program_id(1)
    T = MIX_TILE
    NB = T // BLOCK
    kbufs = (k0, k1, k2, k3)
    vtbufs = (vt0, vt1, vt2, vt3)

    @pl.when(t == 0)
    def _():
        state_ref[...] = jnp.zeros_like(state_ref)
        for buf in vtbufs:
            buf[:, 0:BLOCK] = jnp.zeros((KA, BLOCK), BF16)
        for buf in kbufs:
            buf[0:BLOCK, :] = jnp.zeros((BLOCK, KA), BF16)

    x = x_ref[...]
    xb = x.astype(BF16)

    za = _dot(xb, win_ref[:, C_QA:C_QR])
    q = za[:, 0:QA] * (HEAD_DIM_A ** -0.5)
    for c in range(NB):
        for p in range(N_HEADS_A // 2):
            col = c * 2 * BLOCK + (p % 2) * BLOCK
            qt_s[p // 2, :, col:col + BLOCK] = (
                q[c * BLOCK:(c + 1) * BLOCK, p * KA:(p + 1) * KA].T.astype(BF16))
    ka = za[:, QA:QA + KA]
    va = za[:, QA + KA:QA + 2 * KA]
    kwin_ref[...] = ka[T - WINDOW:, :]
    vwin_ref[...] = va[T - WINDOW:, :]
    ka_r = pltpu.roll(ka, HEAD_DIM_A, 1)
    low = lax.broadcasted_iota(jnp.int32, (T, KA), 1) < HEAD_DIM_A
    k0[BLOCK:, :] = jnp.where(low, ka, 0.0).astype(BF16)
    k1[BLOCK:, :] = jnp.where(low, 0.0, ka_r).astype(BF16)
    k2[BLOCK:, :] = jnp.where(low, ka_r, 0.0).astype(BF16)
    k3[BLOCK:, :] = jnp.where(low, 0.0, ka).astype(BF16)
    va_t = va.T
    va_tr = pltpu.roll(va_t, HEAD_DIM_A, 0)
    top = lax.broadcasted_iota(jnp.int32, (KA, T), 0) < HEAD_DIM_A
    vt0[:, BLOCK:] = jnp.where(top, va_t, 0.0).astype(BF16)
    vt1[:, BLOCK:] = jnp.where(top, 0.0, va_tr).astype(BF16)
    vt2[:, BLOCK:] = jnp.where(top, va_tr, 0.0).astype(BF16)
    vt3[:, BLOCK:] = jnp.where(top, 0.0, va_t).astype(BF16)

    first_slab = lax.broadcasted_iota(jnp.int32, (1, 2 * BLOCK), 1) < BLOCK
    sink_row = []
    for g in range(N_KV_A):
        for e in range(2):
            sink_row.append(jnp.where(first_slab, sinks_ref[layer, 4 * g + e], sinks_ref[layer, 4 * g + 2 + e]))
    top_o = lax.broadcasted_iota(jnp.int32, (KA, 2 * BLOCK), 0) < HEAD_DIM_A

    def scores(c):
        out = []
        for g in range(N_KV_A):
            qg_t = qt_s[g, :, c * 2 * BLOCK:(c + 1) * 2 * BLOCK]
            for e in range(2):
                out.append(_dot(kbufs[2 * g + e][c * BLOCK:(c + 2) * BLOCK, :], qg_t))
        return out

    def softmax(c, s_list):
        for ge in range(4):
            if c == 0:
                bias = bias_ref[jnp.where(t == 0, 4, 0) + ge]
            else:
                bias = bias_ref[ge]
            s = s_list[ge] + bias
            m = jnp.maximum(jnp.max(s, axis=0, keepdims=True), sink_row[ge])
            pe = jnp.exp(s - m)
            den = jnp.sum(pe, axis=0, keepdims=True) + jnp.exp(sink_row[ge] - m)
            p_s[c * 4 + ge] = pe.astype(BF16)
            inv_s[c * 4 + ge] = jnp.broadcast_to(1.0 / den, (8, 2 * BLOCK))

    def attend(c):
        for g in range(N_KV_A):
            o_t = (_dot(vtbufs[2 * g][:, c * BLOCK:(c + 2) * BLOCK], p_s[c * 4 + 2 * g])
                   + _dot(vtbufs[2 * g + 1][:, c * BLOCK:(c + 2) * BLOCK], p_s[c * 4 + 2 * g + 1]))
            o_t = o_t * jnp.where(top_o, inv_s[c * 4 + 2 * g][0:1, :], inv_s[c * 4 + 2 * g + 1][0:1, :])
            oa_s[c * BLOCK:(c + 1) * BLOCK, 2 * g * KA:(2 * g + 1) * KA] = o_t[:, 0:BLOCK].T.astype(BF16)
            oa_s[c * BLOCK:(c + 1) * BLOCK, (2 * g + 1) * KA:(2 * g + 2) * KA] = o_t[:, BLOCK:].T.astype(BF16)

    RC = RET_CHUNK
    NR = T // RC
    g_chunk = [math.exp(RC * lg) for lg in LOG_GAMMA]

    def ret_local(zr, vr, cc):
        sc, upd = [], []
        for h in range(N_HEADS_R):
            qh = zr[cc * RC:(cc + 1) * RC, DK_R * h:DK_R * (h + 1)].astype(BF16)
            kf = zr[cc * RC:(cc + 1) * RC, QR + DK_R * h:QR + DK_R * (h + 1)] * (DK_R ** -0.5)
            vh = vr[cc * RC:(cc + 1) * RC, DV_R * h:DV_R * (h + 1)]
            sc.append(_dot_nt(qh, kf.astype(BF16)))
            upd.append(_dot((kf * kdec_ref[h]).T.astype(BF16), vh))
        return sc, upd

    def ret_out(zr, vr, gr, cc, sc, s_in):
        for h in range(N_HEADS_R):
            qh = zr[cc * RC:(cc + 1) * RC, DK_R * h:DK_R * (h + 1)].astype(BF16)
            vh = vr[cc * RC:(cc + 1) * RC, DV_R * h:DV_R * (h + 1)]
            o = (_dot((sc[h] * dec_ref[h]).astype(BF16), vh)
                 + _dot(qh, s_in[h].astype(BF16)) * rowdec_ref[h])
            mu = jnp.mean(o, axis=-1, keepdims=True)
            d = o - mu
            var = jnp.mean(d * d, axis=-1, keepdims=True)
            on = d * lax.rsqrt(var + GN_EPS)
            gate = _silu(gr[cc * RC:(cc + 1) * RC, DV_R * h:DV_R * (h + 1)])
            orf_s[cc * RC:(cc + 1) * RC, DV_R * h:DV_R * (h + 1)] = (
                on * gn_ref[:, DV_R * h:DV_R * (h + 1)] * gate).astype(BF16)

    s_all = [scores(c) for c in range(NB)]
    ga_s[...] = jax.nn.sigmoid(_dot(xb, win_ref[:, C_GATE:C_GATE + D_MODEL]))
    for c in range(NB):
        softmax(c, s_all[c])
    gr_s[...] = jax.nn.sigmoid(_dot(xb, win_ref[:, C_GATE + D_MODEL:IN_COLS]))
    zr = _dot(xb, win_ref[:, C_QR:C_VR])
    vr = _dot(xb, win_ref[:, C_VR:C_GR]).astype(BF16)
    gr = _dot(xb, win_ref[:, C_GR:C_GATE])
    for c in range(NB):
        attend(c)
    for buf in vtbufs:
        buf[:, 0:BLOCK] = buf[:, T:T + BLOCK]
    for buf in kbufs:
        buf[0:BLOCK, :] = buf[T:T + BLOCK, :]

    local = [ret_local(zr, vr, cc) for cc in range(NR)]
    states = [[state_ref[h] for h in range(N_HEADS_R)]]
    for cc in range(NR):
        states.append([g_chunk[h] * states[cc][h] + local[cc][1][h] for h in range(N_HEADS_R)])
    for h in range(N_HEADS_R):
        state_ref[h] = states[NR][h]
    ya = _dot(oa_s[...], wa_ref[...])
    for cc in range(NR):
        rows = slice(cc * RC, (cc + 1) * RC)
        ret_out(zr, vr, gr, cc, local[cc][0], states[cc])
        merged = ga_s[rows, :] * ya[rows, :] + gr_s[rows, :] * _dot(orf_s[rows, :], wr_ref[...])
        y = _dot(merged.astype(BF16), wo_ref[...])
        y_ref[rows, :] = _layer_norm(ALPHA * x[rows, :] + y, g_ref[...], b_ref[...])


def _attention_bias():
    i = np.arange(BLOCK)[:, None]
    j = np.arange(2 * BLOCK)[None, :]
    dist = i + BLOCK - j
    band = (dist >= 0) & (dist <= WINDOW)
    out = np.zeros((8, 2 * BLOCK, 2 * BLOCK), np.float32)
    for first in range(2):
        valid = band & ((j >= BLOCK) | (first == 0))
        for g in range(N_KV_A):
            for e in range(2):
                for half in range(2):
                    h = 4 * g + 2 * half + e
                    out[4 * first + 2 * g + e, :, half * BLOCK:(half + 1) * BLOCK] = np.where(
                        valid, -SLOPES[h] * dist, NEG).T
    return jnp.asarray(out)


def _retention_tables():
    pos = np.arange(RET_CHUNK, dtype=np.float64)
    diff = pos[:, None] - pos[None, :]
    lg = np.asarray(LOG_GAMMA)[:, None, None]
    dec = np.where(diff >= 0, np.exp(np.maximum(diff, 0.0) * lg), 0.0)
    rowdec = np.broadcast_to(np.exp((pos + 1.0)[None, :, None] * lg), (N_HEADS_R, RET_CHUNK, DV_R))
    kdec = np.broadcast_to(np.exp((RET_CHUNK - 1.0 - pos)[None, :, None] * lg),
                           (N_HEADS_R, RET_CHUNK, DK_R))
    return (jnp.asarray(dec, F32), jnp.asarray(rowdec, F32), jnp.asarray(kdec, F32))


def _mix_prompt(x, sinks, w_in, gn, wa, wr, wo, g, b, layer, side=()):
    B, S, _ = x.shape
    T = MIX_TILE
    nt = S // T
    bias = _attention_bias()
    dec, rowdec, kdec = _retention_tables()
    s_in, s_out, s_shape, s_args = _side_specs(side, lambda bi, ti: bi * nt + ti, B * nt)
    outs = pl.pallas_call(
        functools.partial(_mix_prompt_body, layer=layer, n_side=len(side)),
        grid=(B, nt),
        in_specs=[
            pl.BlockSpec(memory_space=pltpu.SMEM),
            pl.BlockSpec((None, T, D_MODEL), lambda bi, ti: (bi, ti, 0)),
            _whole(w_in),
            _whole(bias), _whole(dec), _whole(rowdec), _whole(kdec),
            _resident((1, VR), layer),
            _whole(wa),
            _whole(wr),
            _whole(wo),
            _resident((1, D_MODEL), layer),
            _resident((1, D_MODEL), layer),
        ] + s_in,
        out_specs=[
            pl.BlockSpec((None, T, D_MODEL), lambda bi, ti: (bi, ti, 0)),
            pl.BlockSpec((None, WINDOW, KA), lambda bi, ti: (bi, 0, 0)),
            pl.BlockSpec((None, WINDOW, KA), lambda bi, ti: (bi, 0, 0)),
            pl.BlockSpec((None, N_HEADS_R, DK_R, DV_R), lambda bi, ti: (bi, 0, 0, 0)),
        ] + s_out,
        out_shape=[
            jax.ShapeDtypeStruct((B, S, D_MODEL), F32),
            jax.ShapeDtypeStruct((B, WINDOW, KA), F32),
            jax.ShapeDtypeStruct((B, WINDOW, KA), F32),
            jax.ShapeDtypeStruct((B, N_HEADS_R, DK_R, DV_R), F32),
        ] + s_shape,
        scratch_shapes=(
            [pltpu.VMEM((N_KV_A, KA, T * 2), BF16)]
            + [pltpu.VMEM((KA, T + BLOCK), BF16) for _ in range(4)]
            + [pltpu.VMEM((T + BLOCK, KA), BF16) for _ in range(4)]
            + [pltpu.VMEM((T, QA), BF16), pltpu.VMEM((T, VR), BF16),
               pltpu.VMEM((T // BLOCK * 4, 2 * BLOCK, 2 * BLOCK), BF16),
               pltpu.VMEM((T // BLOCK * 4, 8, 2 * BLOCK), F32),
               pltpu.VMEM((T, D_MODEL), F32), pltpu.VMEM((T, D_MODEL), F32)]),
        compiler_params=pltpu.CompilerParams(
            dimension_semantics=("arbitrary", "arbitrary"), vmem_limit_bytes=VMEM_LIMIT),
        name="mix_prompt",
    )(sinks, x, w_in, bias, dec, rowdec, kdec, gn, wa, wr, wo, g, b, *s_args)
    return outs[0], outs[1], outs[2], outs[3], outs[4:]


def _head_placement():
    e = np.zeros((QA, N_HEADS_A * KA), np.float32)
    for h in range(N_HEADS_A):
        g = h // GROUP_A
        for d in range(HEAD_DIM_A):
            e[h * HEAD_DIM_A + d, h * KA + g * HEAD_DIM_A + d] = 1.0
    return e


def _smp_proj_body(x_ref, win_ref, e_ref, qm_ref, kvn_ref, zt_ref, vr_ref, zg_ref):
    z = _dot(x_ref[...].astype(BF16), win_ref[...])
    qa = (z[:, 0:QA] * (HEAD_DIM_A ** -0.5)).astype(BF16)
    qm_ref[...] = _dot(qa, e_ref[...])
    kvn_ref[...] = z[:, C_KA:C_QR]
    zt_ref[...] = z[:, C_QR:C_VR].T
    vr_ref[...] = z[:, C_VR:C_GR]
    zg_ref[...] = z[:, C_GR:]


def _smp_proj(x, w_in, e, layer):
    n = x.shape[0]
    widths = [(n, N_HEADS_A * KA), (n, 2 * KA), (2 * QR, n), (n, VR), (n, IN_COLS - C_GR)]
    return pl.pallas_call(
        _smp_proj_body,
        grid=(1,),
        in_specs=[
            pl.BlockSpec((n, D_MODEL), lambda i: (0, 0)),
            _whole(w_in),
            pl.BlockSpec((QA, N_HEADS_A * KA), lambda i: (0, 0)),
        ],
        out_specs=[pl.BlockSpec(s, lambda i: (0, 0)) for s in widths],
        out_shape=[jax.ShapeDtypeStruct(s, F32) for s in widths],
        compiler_params=pltpu.CompilerParams(
            dimension_semantics=("arbitrary",), vmem_limit_bytes=VMEM_LIMIT),
        name="smp_proj",
    )(x, w_in, e)


def _smp_core_body(sinks_ref, qm_ref, kvn_ref, ck_ref, cv_ref, zt_ref, vr_ref, st_ref, *rest, layer):
    o3_ref, or_ref, nk_ref, nv_ref, ns_ref = rest[-5:]
    row8 = lax.broadcasted_iota(jnp.int32, (N_HEADS_A, KA), 0)
    lane8 = lax.broadcasted_iota(jnp.int32, (N_HEADS_A, KA), 1)
    slope = jnp.zeros((N_HEADS_A, KA), F32)
    sinkv = jnp.zeros((N_HEADS_A, KA), F32)
    for h in range(N_HEADS_A):
        slope = jnp.where(row8 == h, SLOPES[h], slope)
        sinkv = jnp.where(row8 == h, sinks_ref[layer, h], sinkv)
    sink = sinkv[:, 0:1]
    bias = slope * (WINDOW - lane8).astype(F32)
    in_group = (lane8 < HEAD_DIM_A) == (row8 < GROUP_A)
    last_row = lax.broadcasted_iota(jnp.int32, (WINDOW, KA), 0) == WINDOW - 1
    zt = zt_ref[...]
    scores = [_dot_nt(qm_ref[bl].astype(BF16), ck_ref[bl].astype(BF16)) for bl in range(SMP_TILE)]
    probs = []
    for bl in range(SMP_TILE):
        qm = qm_ref[bl]
        kn = kvn_ref[bl:bl + 1, 0:KA]
        s = scores[bl] - bias
        s_new = jnp.sum(qm * kn, axis=-1, keepdims=True)
        m = jnp.maximum(jnp.maximum(jnp.max(s, axis=-1, keepdims=True), s_new), sink)
        p = jnp.exp(s - m)
        p_new = jnp.exp(s_new - m)
        inv = 1.0 / (jnp.sum(p, axis=-1, keepdims=True) + p_new + jnp.exp(sink - m))
        probs.append(((p * inv).astype(BF16), p_new * inv))
    for bl in range(SMP_TILE):
        kb = ck_ref[bl]
        vb = cv_ref[bl]
        kn = kvn_ref[bl:bl + 1, 0:KA]
        vn = kvn_ref[bl:bl + 1, KA:2 * KA]
        o = _dot(probs[bl][0], vb.astype(BF16)) + probs[bl][1] * vn
        o3_ref[bl] = jnp.where(in_group, o, 0.0)
        nk_ref[bl] = jnp.where(last_row, kn, pltpu.roll(kb, WINDOW - 1, 0))
        nv_ref[bl] = jnp.where(last_row, vn, pltpu.roll(vb, WINDOW - 1, 0))
    for bl in range(SMP_TILE):
        for h in range(N_HEADS_R):
            gamma = math.exp(LOG_GAMMA[h])
            qc = zt[DK_R * h:DK_R * (h + 1), bl:bl + 1]
            kc = zt[QR + DK_R * h:QR + DK_R * (h + 1), bl:bl + 1] * (DK_R ** -0.5)
            s0 = st_ref[bl, h]
            v = vr_ref[bl:bl + 1, DV_R * h:DV_R * (h + 1)]
            sc = jnp.sum(qc * kc, axis=0, keepdims=True)
            o_r = sc * v + jnp.sum(qc * s0, axis=0, keepdims=True) * gamma
            ns_ref[bl, h] = gamma * s0 + kc * v
            or_ref[bl:bl + 1, DV_R * h:DV_R * (h + 1)] = o_r


def _smp_core(sinks, qm3, kvn, ck, cv, zt, vr, st, stacked, layer):
    n = qm3.shape[0]
    tb = SMP_TILE
    in_specs = [
        pl.BlockSpec(memory_space=pltpu.SMEM),
        pl.BlockSpec((tb, N_HEADS_A, KA), lambda i: (i, 0, 0)),
        pl.BlockSpec((tb, 2 * KA), lambda i: (i, 0)),
        pl.BlockSpec((None, tb, WINDOW, KA), lambda i: (layer, i, 0, 0)),
        pl.BlockSpec((None, tb, WINDOW, KA), lambda i: (layer, i, 0, 0)),
        pl.BlockSpec((None, 2 * QR, tb), lambda i: (i, 0, 0)),
        pl.BlockSpec((tb, VR), lambda i: (i, 0)),
        pl.BlockSpec((None, tb, N_HEADS_R, DK_R, DV_R), lambda i: (layer, i, 0, 0, 0)),
    ]
    args = [sinks, qm3, kvn, ck, cv, zt, vr, st]
    aliases = {}
    if stacked is not None:
        for k, buf in enumerate(stacked):
            aliases[len(args)] = 2 + k
            in_specs.append(pl.BlockSpec(memory_space=pl.ANY))
            args.append(buf)
    return pl.pallas_call(
        functools.partial(_smp_core_body, layer=layer),
        grid=(n // tb,),
        in_specs=in_specs,
        out_specs=[
            pl.BlockSpec((tb, N_HEADS_A, KA), lambda i: (i, 0, 0)),
            pl.BlockSpec((tb, VR), lambda i: (i, 0)),
            pl.BlockSpec((None, tb, WINDOW, KA), lambda i: (layer, i, 0, 0)),
            pl.BlockSpec((None, tb, WINDOW, KA), lambda i: (layer, i, 0, 0)),
            pl.BlockSpec((None, tb, N_HEADS_R, DK_R, DV_R), lambda i: (layer, i, 0, 0, 0)),
        ],
        out_shape=[
            jax.ShapeDtypeStruct((n, N_HEADS_A, KA), F32),
            jax.ShapeDtypeStruct((n, VR), F32),
            jax.ShapeDtypeStruct((DEPTH, n, WINDOW, KA), F32),
            jax.ShapeDtypeStruct((DEPTH, n, WINDOW, KA), F32),
            jax.ShapeDtypeStruct((DEPTH, n, N_HEADS_R, DK_R, DV_R), F32),
        ],
        input_output_aliases=aliases,
        compiler_params=pltpu.CompilerParams(
            dimension_semantics=("arbitrary",), vmem_limit_bytes=VMEM_LIMIT),
        name="smp_core",
    )(*args)


def _smp_out_body(x_ref, zg_ref, o3_ref, or_ref, et_ref, gn_ref, wa_ref, wr_ref, wo_ref,
                  g_ref, b_ref, y_ref):
    x = x_ref[...]
    oa = _dot(o3_ref[...].astype(BF16), et_ref[...]).astype(BF16)
    gr = zg_ref[:, 0:VR]
    parts = []
    for h in range(N_HEADS_R):
        o = or_ref[:, DV_R * h:DV_R * (h + 1)]
        mu = jnp.mean(o, axis=-1, keepdims=True)
        d = o - mu
        var = jnp.mean(d * d, axis=-1, keepdims=True)
        parts.append(d * lax.rsqrt(var + GN_EPS))
    orn = jnp.concatenate(parts, axis=-1)
    orf = (orn * gn_ref[...] * _silu(gr)).astype(BF16)
    merged = (jax.nn.sigmoid(zg_ref[:, VR:VR + D_MODEL]) * _dot(oa, wa_ref[...])
              + jax.nn.sigmoid(zg_ref[:, VR + D_MODEL:]) * _dot(orf, wr_ref[...]))
    y = _dot(merged.astype(BF16), wo_ref[...])
    y_ref[...] = _layer_norm(ALPHA * x + y, g_ref[...], b_ref[...])


def _smp_out(x, zg, o3, o_r, et, gn, wa, wr, wo, g, b, layer):
    n = x.shape[0]
    whole = lambda a: pl.BlockSpec(a.shape, lambda i: (0,) * a.ndim)
    return pl.pallas_call(
        _smp_out_body,
        grid=(1,),
        in_specs=[
            whole(x), whole(zg), whole(o3), whole(o_r), whole(et),
            _resident((1, VR), layer),
            _whole(wa),
            _whole(wr),
            _whole(wo),
            _resident((1, D_MODEL), layer),
            _resident((1, D_MODEL), layer),
        ],
        out_specs=pl.BlockSpec((n, D_MODEL), lambda i: (0, 0)),
        out_shape=jax.ShapeDtypeStruct((n, D_MODEL), F32),
        compiler_params=pltpu.CompilerParams(
            dimension_semantics=("arbitrary",), vmem_limit_bytes=VMEM_LIMIT),
        name="smp_out",
    )(x, zg, o3, o_r, et, gn, wa, wr, wo, g, b)


def _mix_sample(x, sinks, w_in, cache_k, cache_v, state, stacked, gn, wa, wr, wo, g, b, e, et, layer):
    n = x.shape[0]
    qm, kvn, zt, vr, zg = _smp_proj(x, w_in, e, layer)
    qm3 = qm.reshape(n, N_HEADS_A, KA)
    zt = zt.reshape(2 * QR, n // SMP_TILE, SMP_TILE).transpose(1, 0, 2)
    o3, o_r, nk, nv, ns = _smp_core(sinks, qm3, kvn, cache_k, cache_v, zt, vr, state, stacked, layer)
    y = _smp_out(x, zg, o3.reshape(n, N_HEADS_A * KA), o_r, et, gn, wa, wr, wo, g, b, layer)
    return y, (nk, nv, ns)


def kernel(x_prompt, x_sample, cache_win_k, cache_win_v, state_ret, w_ff1_gu, w_ff1_dn, ln1_g, ln1_b, w_in, attn_sinks, ret_gn_g, w_br_a, w_br_r, w_o, ln2_g, ln2_b, w_ff2_gu, w_ff2_dn, ln3_g, ln3_b):
    B, S, _ = x_prompt.shape
    n = x_sample.shape[0]
    row = lambda p: p.reshape(DEPTH, 1, -1)
    g1, b1, g2, b2, g3, b3 = row(ln1_g), row(ln1_b), row(ln2_g), row(ln2_b), row(ln3_g), row(ln3_b)
    gn = row(ret_gn_g)
    e_np = _head_placement()
    e = jnp.asarray(e_np, BF16)
    et = jnp.asarray(e_np.T, BF16)
    ck = cache_win_k.reshape(DEPTH, n, WINDOW, KA)
    cv = cache_win_v.reshape(DEPTH, n, WINDOW, KA)

    xp = x_prompt.reshape(B * S, D_MODEL)
    xs = x_sample.reshape(n, D_MODEL)
    pk, pv, pr = [], [], []
    stacked = None
    ffn_steps = B * S // FFN_TILE
    mix_steps = B * S // MIX_TILE
    ffn1_w = (w_ff1_gu[0].astype(BF16), w_ff1_dn[0].astype(BF16))
    for l in range(DEPTH):
        xp, xs, (win, wa, wr, wo) = _ffn(
            xp, xs, *ffn1_w, g1, b1, l, FFN_TILE,
            side=[_side(w, l, ffn_steps) for w in (w_in, w_br_a, w_br_r, w_o)])
        yp, kw, vw, st, ffn2_w = _mix_prompt(
            xp.reshape(B, S, D_MODEL), attn_sinks, win, gn, wa, wr, wo, g2, b2, l,
            side=[_side(w, l, mix_steps) for w in (w_ff2_gu, w_ff2_dn)])
        xp = yp.reshape(B * S, D_MODEL)
        pk.append(kw)
        pv.append(vw)
        pr.append(st)
        xs, stacked = _mix_sample(xs, attn_sinks, win, ck, cv, state_ret, stacked, gn, wa, wr, wo,
                                  g2, b2, e, et, l)
        nxt = [_side(w, l + 1, ffn_steps) for w in (w_ff1_gu, w_ff1_dn)] if l + 1 < DEPTH else []
        xp, xs, ffn1_w = _ffn(xp, xs, *ffn2_w, g3, b3, l, FFN_TILE, side=nxt)
    win_shape = (DEPTH, -1, WINDOW, N_KV_A, HEAD_DIM_A)
    return (xp.reshape(B, S, D_MODEL), xs.reshape(n, 1, D_MODEL),
            jnp.stack(pk).reshape(win_shape), jnp.stack(pv).reshape(win_shape), jnp.stack(pr),
            stacked[0].reshape(win_shape), stacked[1].reshape(win_shape), stacked[2])
```

```python
import functools
import math

import numpy as np
import jax
import jax.numpy as jnp
from jax import lax
from jax.experimental import pallas as pl
from jax.experimental.pallas import tpu as pltpu

D_MODEL = 1024
DEPTH = 4
N_HEADS_A = 8
N_KV_A = 2
HEAD_DIM_A = 64
GROUP_A = N_HEADS_A // N_KV_A
WINDOW = 128
BLOCK = 128
N_HEADS_R = 4
DK_R = 128
DV_R = 256
D_FF = 2816
ALPHA = (2.0 * DEPTH) ** 0.25
LN_EPS = 1e-5
GN_EPS = 1e-6
NEG = -1e30

QA = N_HEADS_A * HEAD_DIM_A
KA = N_KV_A * HEAD_DIM_A
QR = N_HEADS_R * DK_R
VR = N_HEADS_R * DV_R
C_QA, C_KA, C_VA = 0, QA, QA + KA
C_QR = QA + 2 * KA
C_KR = C_QR + QR
C_VR = C_KR + QR
C_GR = C_VR + VR
C_GATE = C_GR + VR
IN_COLS = C_GATE + 2 * D_MODEL

SLOPES = [2.0 ** (-8.0 * (h + 1) / N_HEADS_A) for h in range(N_HEADS_A)]
LOG_GAMMA = [math.log1p(-(2.0 ** (-5.0 - h))) for h in range(N_HEADS_R)]

BF16 = jnp.bfloat16
F32 = jnp.float32

FFN_CHUNK = 256
FFN_TILE = 1024
FFN_SUBTILE = 512
FFN_MIN_PIECE = 256
MIX_TILE = 256
RET_CHUNK = 256
VMEM_LIMIT = 60 * 1024 * 1024


def _dot(a, b):
    return jnp.dot(a, b, preferred_element_type=F32)


def _dot_nt(a, b):
    return lax.dot_general(a, b, (((1,), (1,)), ((), ())), preferred_element_type=F32)


def _layer_norm(v, g, b):
    mu = jnp.mean(v, axis=-1, keepdims=True)
    d = v - mu
    var = jnp.mean(d * d, axis=-1, keepdims=True)
    return d * lax.rsqrt(var + LN_EPS) * g + b


def _silu(a):
    return a * jax.nn.sigmoid(a)


def _cast_side(refs, n_side):
    for k in range(n_side):
        refs[n_side + k][...] = refs[k][...].astype(BF16)


BF16_SUBLANES = 16


def _side(w, layer, total):
    r = w.shape[1]
    nblk = total
    while r % nblk or (r // nblk) % BF16_SUBLANES:
        nblk //= 2
    return (w, layer, r // nblk)


def _side_specs(side, step, total):
    in_specs, out_specs, out_shape, args = [], [], [], []
    for w, layer, rows in side:
        _, r, c = w.shape
        nblk = r // rows
        assert r % rows == 0 and total % nblk == 0
        idx = lambda *g, nblk=nblk: step(*g) // (total // nblk)
        in_specs.append(pl.BlockSpec((None, rows, c), lambda *g, idx=idx, layer=layer: (layer, idx(*g), 0)))
        out_specs.append(pl.BlockSpec((rows, c), lambda *g, idx=idx: (idx(*g), 0)))
        out_shape.append(jax.ShapeDtypeStruct((r, c), BF16))
        args.append(w)
    return in_specs, out_specs, out_shape, args


def _down_pieces(tm):
    pieces, left = [], tm
    while left > 2 * FFN_MIN_PIECE:
        pieces.append(left // 2)
        left -= left // 2
    return pieces + ([left // 2, left // 2] if left > FFN_MIN_PIECE else [left])


def _ffn_body(x_ref, xs_ref, wgu_ref, wdn_ref, g_ref, b_ref, *rest, n_side):
    side_in, rest = rest[:n_side], rest[n_side:]
    (o_ref, os_ref), rest = rest[:2], rest[2:]
    side_out, (h_ref,) = rest[:n_side], rest[n_side:]
    _cast_side(side_in + side_out, n_side)

    @pl.when(pl.program_id(0) == 0)
    def _():
        _ffn_rows(xs_ref, os_ref, h_ref, wgu_ref, wdn_ref, g_ref, b_ref)

    _ffn_rows(x_ref, o_ref, h_ref, wgu_ref, wdn_ref, g_ref, b_ref)


def _ffn_rows(x_ref, o_ref, h_ref, wgu_ref, wdn_ref, g_ref, b_ref):
    tm = x_ref.shape[0]
    sub = min(FFN_SUBTILE, tm)
    for s in range(tm // sub):
        rows = slice(s * sub, (s + 1) * sub)
        xb = x_ref[rows, :].astype(BF16)
        for j in range(D_FF // FFN_CHUNK):
            lo = j * FFN_CHUNK
            a = _dot(xb, wgu_ref[:, lo:lo + FFN_CHUNK])
            u = _dot(xb, wgu_ref[:, D_FF + lo:D_FF + lo + FFN_CHUNK])
            h_ref[rows, lo:lo + FFN_CHUNK] = (_silu(a) * u).astype(BF16)
    start = 0
    for size in _down_pieces(tm):
        rows = slice(start, start + size)
        d = _dot(h_ref[rows, :], wdn_ref[...])
        o_ref[rows, :] = _layer_norm(ALPHA * x_ref[rows, :] + 0.5 * d, g_ref[...], b_ref[...])
        start += size


def _resident(shape, layer):
    nd = len(shape)
    return pl.BlockSpec((None,) + shape, lambda *_: (layer,) + (0,) * nd,
                        pipeline_mode=pl.Buffered(1))


def _whole(a):
    return pl.BlockSpec(a.shape, lambda *_: (0,) * a.ndim, pipeline_mode=pl.Buffered(1))


def _ffn(x, xs, wgu, wdn, g, b, layer, tm, side=()):
    m = x.shape[0]
    n = xs.shape[0]
    steps = m // tm
    s_in, s_out, s_shape, s_args = _side_specs(side, lambda i: i, steps)
    outs = pl.pallas_call(
        functools.partial(_ffn_body, n_side=len(side)),
        grid=(steps,),
        in_specs=[
            pl.BlockSpec((tm, D_MODEL), lambda i: (i, 0)),
            pl.BlockSpec((n, D_MODEL), lambda i: (0, 0)),
            _whole(wgu),
            _whole(wdn),
            _resident((1, D_MODEL), layer),
            _resident((1, D_MODEL), layer),
        ] + s_in,
        out_specs=[pl.BlockSpec((tm, D_MODEL), lambda i: (i, 0)),
                   pl.BlockSpec((n, D_MODEL), lambda i: (0, 0))] + s_out,
        out_shape=[jax.ShapeDtypeStruct((m, D_MODEL), F32),
                   jax.ShapeDtypeStruct((n, D_MODEL), F32)] + s_shape,
        scratch_shapes=[pltpu.VMEM((tm, D_FF), BF16)],
        compiler_params=pltpu.CompilerParams(
            dimension_semantics=("arbitrary",), vmem_limit_bytes=VMEM_LIMIT),
        name="ffn",
    )(x, xs, wgu, wdn, g, b, *s_args)
    return outs[0], outs[1], outs[2:]


def _mix_prompt_body(sinks_ref, x_ref, win_ref, bias_ref, dec_ref, rowdec_ref, kdec_ref, gn_ref,
                     wa_ref, wr_ref, wo_ref, g_ref, b_ref, *rest, layer, n_side, n_alias):
    smp_in, rest = rest[:7], rest[7 + n_alias:]
    side_in, rest = rest[:n_side], rest[n_side:]
    (y_ref, kwin_ref, vwin_ref, state_ref), rest = rest[:4], rest[4:]
    smp_out, rest = rest[:5], rest[5:]
    side_out, rest = rest[:n_side], rest[n_side:]
    qt_s, vt0, vt1, vt2, vt3, k0, k1, k2, k3, oa_s, orf_s, p_s, inv_s, ga_s, gr_s = rest
    _mix_prompt_tile(sinks_ref, x_ref, win_ref, bias_ref, dec_ref, rowdec_ref, kdec_ref, gn_ref,
                     wa_ref, wr_ref, wo_ref, g_ref, b_ref, y_ref, kwin_ref, vwin_ref, state_ref,
                     qt_s, vt0, vt1, vt2, vt3, k0, k1, k2, k3, oa_s, orf_s, p_s, inv_s, ga_s, gr_s, layer=layer)
    _cast_side(side_in + side_out, n_side)
    _smp_core_tile(sinks_ref, *smp_in, *smp_out, layer=layer)


def _mix_prompt_tile(sinks_ref, x_ref, win_ref, bias_ref, dec_ref, rowdec_ref, kdec_ref, gn_ref,
                     wa_ref, wr_ref, wo_ref, g_ref, b_ref,
                     y_ref, kwin_ref, vwin_ref, state_ref,
                     qt_s, vt0, vt1, vt2, vt3, k0, k1, k2, k3, oa_s, orf_s, p_s, inv_s, ga_s, gr_s,
                     *, layer):
    t = pl.program_id(1)
    T = MIX_TILE
    NB = T // BLOCK
    kbufs = (k0, k1, k2, k3)
    vtbufs = (vt0, vt1, vt2, vt3)

    @pl.when(t == 0)
    def _():
        state_ref[...] = jnp.zeros_like(state_ref)
        for buf in vtbufs:
            buf[:, 0:BLOCK] = jnp.zeros((KA, BLOCK), BF16)
        for buf in kbufs:
            buf[0:BLOCK, :] = jnp.zeros((BLOCK, KA), BF16)

    x = x_ref[...]
    xb = x.astype(BF16)

    za = _dot(xb, win_ref[:, C_QA:C_QR])
    q = za[:, 0:QA] * (HEAD_DIM_A ** -0.5)
    for c in range(NB):
        for p in range(N_HEADS_A // 2):
            col = c * 2 * BLOCK + (p % 2) * BLOCK
            qt_s[p // 2, :, col:col + BLOCK] = (
                q[c * BLOCK:(c + 1) * BLOCK, p * KA:(p + 1) * KA].T.astype(BF16))
    ka = za[:, QA:QA + KA]
    va = za[:, QA + KA:QA + 2 * KA]
    kwin_ref[...] = ka[T - WINDOW:, :]
    vwin_ref[...] = va[T - WINDOW:, :]
    ka_r = pltpu.roll(ka, HEAD_DIM_A, 1)
    low = lax.broadcasted_iota(jnp.int32, (T, KA), 1) < HEAD_DIM_A
    k0[BLOCK:, :] = jnp.where(low, ka, 0.0).astype(BF16)
    k1[BLOCK:, :] = jnp.where(low, 0.0, ka_r).astype(BF16)
    k2[BLOCK:, :] = jnp.where(low, ka_r, 0.0).astype(BF16)
    k3[BLOCK:, :] = jnp.where(low, 0.0, ka).astype(BF16)
    va_t = va.T
    va_tr = pltpu.roll(va_t, HEAD_DIM_A, 0)
    top = lax.broadcasted_iota(jnp.int32, (KA, T), 0) < HEAD_DIM_A
    vt0[:, BLOCK:] = jnp.where(top, va_t, 0.0).astype(BF16)
    vt1[:, BLOCK:] = jnp.where(top, 0.0, va_tr).astype(BF16)
    vt2[:, BLOCK:] = jnp.where(top, va_tr, 0.0).astype(BF16)
    vt3[:, BLOCK:] = jnp.where(top, 0.0, va_t).astype(BF16)

    first_slab = lax.broadcasted_iota(jnp.int32, (1, 2 * BLOCK), 1) < BLOCK
    sink_row = []
    for g in range(N_KV_A):
        for e in range(2):
            sink_row.append(jnp.where(first_slab, sinks_ref[layer, 4 * g + e], sinks_ref[layer, 4 * g + 2 + e]))
    top_o = lax.broadcasted_iota(jnp.int32, (KA, 2 * BLOCK), 0) < HEAD_DIM_A

    def scores(c):
        out = []
        for g in range(N_KV_A):
            qg_t = qt_s[g, :, c * 2 * BLOCK:(c + 1) * 2 * BLOCK]
            for e in range(2):
                out.append(_dot(kbufs[2 * g + e][c * BLOCK:(c + 2) * BLOCK, :], qg_t))
        return out

    def softmax(c, s_list):
        for ge in range(4):
            if c == 0:
                bias = bias_ref[jnp.where(t == 0, 4, 0) + ge]
            else:
                bias = bias_ref[ge]
            s = s_list[ge] + bias
            m = jnp.maximum(jnp.max(s, axis=0, keepdims=True), sink_row[ge])
            pe = jnp.exp(s - m)
            den = jnp.sum(pe, axis=0, keepdims=True) + jnp.exp(sink_row[ge] - m)
            p_s[c * 4 + ge] = pe.astype(BF16)
            inv_s[c * 4 + ge] = jnp.broadcast_to(1.0 / den, (8, 2 * BLOCK))

    def attend(c):
        for g in range(N_KV_A):
            o_t = (_dot(vtbufs[2 * g][:, c * BLOCK:(c + 2) * BLOCK], p_s[c * 4 + 2 * g])
                   + _dot(vtbufs[2 * g + 1][:, c * BLOCK:(c + 2) * BLOCK], p_s[c * 4 + 2 * g + 1]))
            o_t = o_t * jnp.where(top_o, inv_s[c * 4 + 2 * g][0:1, :], inv_s[c * 4 + 2 * g + 1][0:1, :])
            oa_s[c * BLOCK:(c + 1) * BLOCK, 2 * g * KA:(2 * g + 1) * KA] = o_t[:, 0:BLOCK].T.astype(BF16)
            oa_s[c * BLOCK:(c + 1) * BLOCK, (2 * g + 1) * KA:(2 * g + 2) * KA] = o_t[:, BLOCK:].T.astype(BF16)

    RC = RET_CHUNK
    NR = T // RC
    g_chunk = [math.exp(RC * lg) for lg in LOG_GAMMA]

    def ret_local(zr, vr, cc):
        sc, upd = [], []
        for h in range(N_HEADS_R):
            qh = zr[cc * RC:(cc + 1) * RC, DK_R * h:DK_R * (h + 1)].astype(BF16)
            kf = zr[cc * RC:(cc + 1) * RC, QR + DK_R * h:QR + DK_R * (h + 1)] * (DK_R ** -0.5)
            vh = vr[cc * RC:(cc + 1) * RC, DV_R * h:DV_R * (h + 1)]
            sc.append(_dot_nt(qh, kf.astype(BF16)))
            upd.append(_dot((kf * kdec_ref[h]).T.astype(BF16), vh))
        return sc, upd

    def ret_out(zr, vr, gr, cc, sc, s_in):
        for h in range(N_HEADS_R):
            qh = zr[cc * RC:(cc + 1) * RC, DK_R * h:DK_R * (h + 1)].astype(BF16)
            vh = vr[cc * RC:(cc + 1) * RC, DV_R * h:DV_R * (h + 1)]
            o = (_dot((sc[h] * dec_ref[h]).astype(BF16), vh)
                 + _dot(qh, s_in[h].astype(BF16)) * rowdec_ref[h])
            mu = jnp.mean(o, axis=-1, keepdims=True)
            d = o - mu
            var = jnp.mean(d * d, axis=-1, keepdims=True)
            on = d * lax.rsqrt(var + GN_EPS)
            gate = _silu(gr[cc * RC:(cc + 1) * RC, DV_R * h:DV_R * (h + 1)])
            orf_s[cc * RC:(cc + 1) * RC, DV_R * h:DV_R * (h + 1)] = (
                on * gn_ref[:, DV_R * h:DV_R * (h + 1)] * gate).astype(BF16)

    s_all = [scores(c) for c in range(NB)]
    ga_s[...] = jax.nn.sigmoid(_dot(xb, win_ref[:, C_GATE:C_GATE + D_MODEL]))
    for c in range(NB):
        softmax(c, s_all[c])
    gr_s[...] = jax.nn.sigmoid(_dot(xb, win_ref[:, C_GATE + D_MODEL:IN_COLS]))
    zr = _dot(xb, win_ref[:, C_QR:C_VR])
    vr = _dot(xb, win_ref[:, C_VR:C_GR]).astype(BF16)
    gr = _dot(xb, win_ref[:, C_GR:C_GATE])
    for c in range(NB):
        attend(c)
    for buf in vtbufs:
        buf[:, 0:BLOCK] = buf[:, T:T + BLOCK]
    for buf in kbufs:
        buf[0:BLOCK, :] = buf[T:T + BLOCK, :]

    local = [ret_local(zr, vr, cc) for cc in range(NR)]
    states = [[state_ref[h] for h in range(N_HEADS_R)]]
    for cc in range(NR):
        states.append([g_chunk[h] * states[cc][h] + local[cc][1][h] for h in range(N_HEADS_R)])
    for h in range(N_HEADS_R):
        state_ref[h] = states[NR][h]
    ya = _dot(oa_s[...], wa_ref[...])
    for cc in range(NR):
        rows = slice(cc * RC, (cc + 1) * RC)
        ret_out(zr, vr, gr, cc, local[cc][0], states[cc])
        merged = ga_s[rows, :] * ya[rows, :] + gr_s[rows, :] * _dot(orf_s[rows, :], wr_ref[...])
        y = _dot(merged.astype(BF16), wo_ref[...])
        y_ref[rows, :] = _layer_norm(ALPHA * x[rows, :] + y, g_ref[...], b_ref[...])


def _attention_bias():
    i = np.arange(BLOCK)[:, None]
    j = np.arange(2 * BLOCK)[None, :]
    dist = i + BLOCK - j
    band = (dist >= 0) & (dist <= WINDOW)
    out = np.zeros((8, 2 * BLOCK, 2 * BLOCK), np.float32)
    for first in range(2):
        valid = band & ((j >= BLOCK) | (first == 0))
        for g in range(N_KV_A):
            for e in range(2):
                for half in range(2):
                    h = 4 * g + 2 * half + e
                    out[4 * first + 2 * g + e, :, half * BLOCK:(half + 1) * BLOCK] = np.where(
                        valid, -SLOPES[h] * dist, NEG).T
    return jnp.asarray(out)


def _retention_tables():
    pos = np.arange(RET_CHUNK, dtype=np.float64)
    diff = pos[:, None] - pos[None, :]
    lg = np.asarray(LOG_GAMMA)[:, None, None]
    dec = np.where(diff >= 0, np.exp(np.maximum(diff, 0.0) * lg), 0.0)
    rowdec = np.broadcast_to(np.exp((pos + 1.0)[None, :, None] * lg), (N_HEADS_R, RET_CHUNK, DV_R))
    kdec = np.broadcast_to(np.exp((RET_CHUNK - 1.0 - pos)[None, :, None] * lg),
                           (N_HEADS_R, RET_CHUNK, DK_R))
    return (jnp.asarray(dec, F32), jnp.asarray(rowdec, F32), jnp.asarray(kdec, F32))


def _mix_prompt(x, sinks, w_in, gn, wa, wr, wo, g, b, layer, smp, stacked, side=()):
    B, S, _ = x.shape
    T = MIX_TILE
    nt = S // T
    bias = _attention_bias()
    dec, rowdec, kdec = _retention_tables()
    step = lambda bi, ti: bi * nt + ti
    s_in, s_out, s_shape, s_args = _side_specs(side, step, B * nt)
    n = smp[0].shape[0]
    c_in, c_args, n_alias, c_out, c_shape = _smp_core_specs(*smp, stacked, layer, n // (B * nt), step)
    n_fixed = 13
    aliases = {n_fixed + 7 + k: 4 + 2 + k for k in range(n_alias)}
    outs = pl.pallas_call(
        functools.partial(_mix_prompt_body, layer=layer, n_side=len(side), n_alias=n_alias),
        grid=(B, nt),
        in_specs=[
            pl.BlockSpec(memory_space=pltpu.SMEM),
            pl.BlockSpec((None, T, D_MODEL), lambda bi, ti: (bi, ti, 0)),
            _whole(w_in),
            _whole(bias), _whole(dec), _whole(rowdec), _whole(kdec),
            _resident((1, VR), layer),
            _whole(wa),
            _whole(wr),
            _whole(wo),
            _resident((1, D_MODEL), layer),
            _resident((1, D_MODEL), layer),
        ] + c_in + s_in,
        out_specs=[
            pl.BlockSpec((None, T, D_MODEL), lambda bi, ti: (bi, ti, 0)),
            pl.BlockSpec((None, WINDOW, KA), lambda bi, ti: (bi, 0, 0)),
            pl.BlockSpec((None, WINDOW, KA), lambda bi, ti: (bi, 0, 0)),
            pl.BlockSpec((None, N_HEADS_R, DK_R, DV_R), lambda bi, ti: (bi, 0, 0, 0)),
        ] + c_out + s_out,
        out_shape=[
            jax.ShapeDtypeStruct((B, S, D_MODEL), F32),
            jax.ShapeDtypeStruct((B, WINDOW, KA), F32),
            jax.ShapeDtypeStruct((B, WINDOW, KA), F32),
            jax.ShapeDtypeStruct((B, N_HEADS_R, DK_R, DV_R), F32),
        ] + c_shape + s_shape,
        input_output_aliases=aliases,
        scratch_shapes=(
            [pltpu.VMEM((N_KV_A, KA, T * 2), BF16)]
            + [pltpu.VMEM((KA, T + BLOCK), BF16) for _ in range(4)]
            + [pltpu.VMEM((T + BLOCK, KA), BF16) for _ in range(4)]
            + [pltpu.VMEM((T, QA), BF16), pltpu.VMEM((T, VR), BF16),
               pltpu.VMEM((T // BLOCK * 4, 2 * BLOCK, 2 * BLOCK), BF16),
               pltpu.VMEM((T // BLOCK * 4, 8, 2 * BLOCK), F32),
               pltpu.VMEM((T, D_MODEL), F32), pltpu.VMEM((T, D_MODEL), F32)]),
        compiler_params=pltpu.CompilerParams(
            dimension_semantics=("arbitrary", "arbitrary"), vmem_limit_bytes=VMEM_LIMIT),
        name="mix_prompt",
    )(sinks, x, w_in, bias, dec, rowdec, kdec, gn, wa, wr, wo, g, b, *c_args, *s_args)
    return outs[0], outs[1], outs[2], outs[3], outs[4:9], outs[9:]


def _head_placement():
    e = np.zeros((QA, N_HEADS_A * KA), np.float32)
    for h in range(N_HEADS_A):
        g = h // GROUP_A
        for d in range(HEAD_DIM_A):
            e[h * HEAD_DIM_A + d, h * KA + g * HEAD_DIM_A + d] = 1.0
    return e


def _smp_proj_body(x_ref, win_ref, e_ref, qm_ref, kvn_ref, zt_ref, vr_ref, zg_ref):
    z = _dot(x_ref[...].astype(BF16), win_ref[...])
    qa = (z[:, 0:QA] * (HEAD_DIM_A ** -0.5)).astype(BF16)
    qm_ref[...] = _dot(qa, e_ref[...])
    kvn_ref[...] = z[:, C_KA:C_QR]
    zt_ref[...] = z[:, C_QR:C_VR].T
    vr_ref[...] = z[:, C_VR:C_GR]
    zg_ref[...] = z[:, C_GR:]


def _smp_proj(x, w_in, e, layer):
    n = x.shape[0]
    widths = [(n, N_HEADS_A * KA), (n, 2 * KA), (2 * QR, n), (n, VR), (n, IN_COLS - C_GR)]
    return pl.pallas_call(
        _smp_proj_body,
        grid=(1,),
        in_specs=[
            pl.BlockSpec((n, D_MODEL), lambda i: (0, 0)),
            _whole(w_in),
            pl.BlockSpec((QA, N_HEADS_A * KA), lambda i: (0, 0)),
        ],
        out_specs=[pl.BlockSpec(s, lambda i: (0, 0)) for s in widths],
        out_shape=[jax.ShapeDtypeStruct(s, F32) for s in widths],
        compiler_params=pltpu.CompilerParams(
            dimension_semantics=("arbitrary",), vmem_limit_bytes=VMEM_LIMIT),
        name="smp_proj",
    )(x, w_in, e)


def _smp_core_tile(sinks_ref, qm_ref, kvn_ref, ck_ref, cv_ref, zt_ref, vr_ref, st_ref,
                   o3_ref, or_ref, nk_ref, nv_ref, ns_ref, *, layer):
    tb = qm_ref.shape[0]
    row8 = lax.broadcasted_iota(jnp.int32, (N_HEADS_A, KA), 0)
    lane8 = lax.broadcasted_iota(jnp.int32, (N_HEADS_A, KA), 1)
    slope = jnp.zeros((N_HEADS_A, KA), F32)
    sinkv = jnp.zeros((N_HEADS_A, KA), F32)
    for h in range(N_HEADS_A):
        slope = jnp.where(row8 == h, SLOPES[h], slope)
        sinkv = jnp.where(row8 == h, sinks_ref[layer, h], sinkv)
    sink = sinkv[:, 0:1]
    bias = slope * (WINDOW - lane8).astype(F32)
    in_group = (lane8 < HEAD_DIM_A) == (row8 < GROUP_A)
    last_row = lax.broadcasted_iota(jnp.int32, (WINDOW, KA), 0) == WINDOW - 1
    zt = zt_ref[...]
    scores = [_dot_nt(qm_ref[bl].astype(BF16), ck_ref[bl].astype(BF16)) for bl in range(tb)]
    probs = []
    for bl in range(tb):
        qm = qm_ref[bl]
        kn = kvn_ref[bl:bl + 1, 0:KA]
        s = scores[bl] - bias
        s_new = jnp.sum(qm * kn, axis=-1, keepdims=True)
        m = jnp.maximum(jnp.maximum(jnp.max(s, axis=-1, keepdims=True), s_new), sink)
        p = jnp.exp(s - m)
        p_new = jnp.exp(s_new - m)
        inv = 1.0 / (jnp.sum(p, axis=-1, keepdims=True) + p_new + jnp.exp(sink - m))
        probs.append(((p * inv).astype(BF16), p_new * inv))
    for bl in range(tb):
        kb = ck_ref[bl]
        vb = cv_ref[bl]
        kn = kvn_ref[bl:bl + 1, 0:KA]
        vn = kvn_ref[bl:bl + 1, KA:2 * KA]
        o = _dot(probs[bl][0], vb.astype(BF16)) + probs[bl][1] * vn
        o3_ref[bl] = jnp.where(in_group, o, 0.0)
        nk_ref[bl] = jnp.where(last_row, kn, pltpu.roll(kb, WINDOW - 1, 0))
        nv_ref[bl] = jnp.where(last_row, vn, pltpu.roll(vb, WINDOW - 1, 0))
    for bl in range(tb):
        for h in range(N_HEADS_R):
            gamma = math.exp(LOG_GAMMA[h])
            qc = zt[DK_R * h:DK_R * (h + 1), bl:bl + 1]
            kc = zt[QR + DK_R * h:QR + DK_R * (h + 1), bl:bl + 1] * (DK_R ** -0.5)
            s0 = st_ref[bl, h]
            v = vr_ref[bl:bl + 1, DV_R * h:DV_R * (h + 1)]
            sc = jnp.sum(qc * kc, axis=0, keepdims=True)
            o_r = sc * v + jnp.sum(qc * s0, axis=0, keepdims=True) * gamma
            ns_ref[bl, h] = gamma * s0 + kc * v
            or_ref[bl:bl + 1, DV_R * h:DV_R * (h + 1)] = o_r


def _smp_core_specs(qm3, kvn, ck, cv, zt, vr, st, stacked, layer, tb, step):
    n = qm3.shape[0]
    nt = n // tb
    tile3 = lambda *g: (step(*g), 0, 0)
    lay4 = lambda *g: (layer, step(*g), 0, 0)
    lay5 = lambda *g: (layer, step(*g), 0, 0, 0)
    in_specs = [
        pl.BlockSpec((tb, N_HEADS_A, KA), tile3),
        pl.BlockSpec((None, tb, 2 * KA), tile3),
        pl.BlockSpec((None, tb, WINDOW, KA), lay4),
        pl.BlockSpec((None, tb, WINDOW, KA), lay4),
        pl.BlockSpec((None, 2 * QR, tb), tile3),
        pl.BlockSpec((None, tb, VR), tile3),
        pl.BlockSpec((None, tb, N_HEADS_R, DK_R, DV_R), lay5),
    ]
    args = [qm3, kvn.reshape(nt, tb, 2 * KA), ck, cv,
            zt.reshape(2 * QR, nt, tb).transpose(1, 0, 2), vr.reshape(nt, tb, VR), st]
    n_alias = 0
    if stacked is not None:
        n_alias = len(stacked)
        in_specs += [pl.BlockSpec(memory_space=pl.ANY)] * n_alias
        args += list(stacked)
    out_specs = [
        pl.BlockSpec((tb, N_HEADS_A, KA), tile3),
        pl.BlockSpec((None, tb, VR), tile3),
        pl.BlockSpec((None, tb, WINDOW, KA), lay4),
        pl.BlockSpec((None, tb, WINDOW, KA), lay4),
        pl.BlockSpec((None, tb, N_HEADS_R, DK_R, DV_R), lay5),
    ]
    out_shape = [
        jax.ShapeDtypeStruct((n, N_HEADS_A, KA), F32),
        jax.ShapeDtypeStruct((nt, tb, VR), F32),
        jax.ShapeDtypeStruct((DEPTH, n, WINDOW, KA), F32),
        jax.ShapeDtypeStruct((DEPTH, n, WINDOW, KA), F32),
        jax.ShapeDtypeStruct((DEPTH, n, N_HEADS_R, DK_R, DV_R), F32),
    ]
    return in_specs, args, n_alias, out_specs, out_shape


def _smp_out_body(x_ref, zg_ref, o3_ref, or_ref, et_ref, gn_ref, wa_ref, wr_ref, wo_ref,
                  g_ref, b_ref, y_ref):
    x = x_ref[...]
    oa = _dot(o3_ref[...].astype(BF16), et_ref[...]).astype(BF16)
    gr = zg_ref[:, 0:VR]
    parts = []
    for h in range(N_HEADS_R):
        o = or_ref[:, DV_R * h:DV_R * (h + 1)]
        mu = jnp.mean(o, axis=-1, keepdims=True)
        d = o - mu
        var = jnp.mean(d * d, axis=-1, keepdims=True)
        parts.append(d * lax.rsqrt(var + GN_EPS))
    orn = jnp.concatenate(parts, axis=-1)
    orf = (orn * gn_ref[...] * _silu(gr)).astype(BF16)
    merged = (jax.nn.sigmoid(zg_ref[:, VR:VR + D_MODEL]) * _dot(oa, wa_ref[...])
              + jax.nn.sigmoid(zg_ref[:, VR + D_MODEL:]) * _dot(orf, wr_ref[...]))
    y = _dot(merged.astype(BF16), wo_ref[...])
    y_ref[...] = _layer_norm(ALPHA * x + y, g_ref[...], b_ref[...])


def _smp_out(x, zg, o3, o_r, et, gn, wa, wr, wo, g, b, layer):
    n = x.shape[0]
    whole = lambda a: pl.BlockSpec(a.shape, lambda i: (0,) * a.ndim)
    return pl.pallas_call(
        _smp_out_body,
        grid=(1,),
        in_specs=[
            whole(x), whole(zg), whole(o3), whole(o_r), whole(et),
            _resident((1, VR), layer),
            _whole(wa),
            _whole(wr),
            _whole(wo),
            _resident((1, D_MODEL), layer),
            _resident((1, D_MODEL), layer),
        ],
        out_specs=pl.BlockSpec((n, D_MODEL), lambda i: (0, 0)),
        out_shape=jax.ShapeDtypeStruct((n, D_MODEL), F32),
        compiler_params=pltpu.CompilerParams(
            dimension_semantics=("arbitrary",), vmem_limit_bytes=VMEM_LIMIT),
        name="smp_out",
    )(x, zg, o3, o_r, et, gn, wa, wr, wo, g, b)


def kernel(x_prompt, x_sample, cache_win_k, cache_win_v, state_ret, w_ff1_gu, w_ff1_dn, ln1_g, ln1_b, w_in, attn_sinks, ret_gn_g, w_br_a, w_br_r, w_o, ln2_g, ln2_b, w_ff2_gu, w_ff2_dn, ln3_g, ln3_b):
    B, S, _ = x_prompt.shape
    n = x_sample.shape[0]
    row = lambda p: p.reshape(DEPTH, 1, -1)
    g1, b1, g2, b2, g3, b3 = row(ln1_g), row(ln1_b), row(ln2_g), row(ln2_b), row(ln3_g), row(ln3_b)
    gn = row(ret_gn_g)
    e_np = _head_placement()
    e = jnp.asarray(e_np, BF16)
    et = jnp.asarray(e_np.T, BF16)
    ck = cache_win_k.reshape(DEPTH, n, WINDOW, KA)
    cv = cache_win_v.reshape(DEPTH, n, WINDOW, KA)

    xp = x_prompt.reshape(B * S, D_MODEL)
    xs = x_sample.reshape(n, D_MODEL)
    pk, pv, pr = [], [], []
    stacked = None
    ffn_steps = B * S // FFN_TILE
    mix_steps = B * S // MIX_TILE
    ffn1_w = (w_ff1_gu[0].astype(BF16), w_ff1_dn[0].astype(BF16))
    for l in range(DEPTH):
        xp, xs, (win, wa, wr, wo) = _ffn(
            xp, xs, *ffn1_w, g1, b1, l, FFN_TILE,
            side=[_side(w, l, ffn_steps) for w in (w_in, w_br_a, w_br_r, w_o)])
        qm, kvn, zt, vr, zg = _smp_proj(xs, win, e, l)
        smp = (qm.reshape(n, N_HEADS_A, KA), kvn, ck, cv, zt, vr, state_ret)
        yp, kw, vw, st, (o3, o_r, *stacked), ffn2_w = _mix_prompt(
            xp.reshape(B, S, D_MODEL), attn_sinks, win, gn, wa, wr, wo, g2, b2, l, smp, stacked,
            side=[_side(w, l, mix_steps) for w in (w_ff2_gu, w_ff2_dn)])
        xp = yp.reshape(B * S, D_MODEL)
        pk.append(kw)
        pv.append(vw)
        pr.append(st)
        xs = _smp_out(xs, zg, o3.reshape(n, N_HEADS_A * KA), o_r.reshape(n, VR), et, gn, wa, wr, wo,
                      g2, b2, l)
        nxt = [_side(w, l + 1, ffn_steps) for w in (w_ff1_gu, w_ff1_dn)] if l + 1 < DEPTH else []
        xp, xs, ffn1_w = _ffn(xp, xs, *ffn2_w, g3, b3, l, FFN_TILE, side=nxt)
    win_shape = (DEPTH, -1, WINDOW, N_KV_A, HEAD_DIM_A)
    return (xp.reshape(B, S, D_MODEL), xs.reshape(n, 1, D_MODEL),
            jnp.stack(pk).reshape(win_shape), jnp.stack(pv).reshape(win_shape), jnp.stack(pr),
            stacked[0].reshape(win_shape), stacked[1].reshape(win_shape), stacked[2])
```

```python
import functools
import math

import numpy as np
import jax
import jax.numpy as jnp
from jax import lax
from jax.experimental import pallas as pl
from jax.experimental.pallas import tpu as pltpu

D_MODEL = 1024
DEPTH = 4
N_HEADS_A = 8
N_KV_A = 2
HEAD_DIM_A = 64
GROUP_A = N_HEADS_A // N_KV_A
WINDOW = 128
BLOCK = 128
N_HEADS_R = 4
DK_R = 128
DV_R = 256
D_FF = 2816
ALPHA = (2.0 * DEPTH) ** 0.25
LN_EPS = 1e-5
GN_EPS = 1e-6
NEG = -1e30

QA = N_HEADS_A * HEAD_DIM_A
KA = N_KV_A * HEAD_DIM_A
QR = N_HEADS_R * DK_R
VR = N_HEADS_R * DV_R
C_QA, C_KA, C_VA = 0, QA, QA + KA
C_QR = QA + 2 * KA
C_KR = C_QR + QR
C_VR = C_KR + QR
C_GR = C_VR + VR
C_GATE = C_GR + VR
IN_COLS = C_GATE + 2 * D_MODEL

SLOPES = [2.0 ** (-8.0 * (h + 1) / N_HEADS_A) for h in range(N_HEADS_A)]
LOG_GAMMA = [math.log1p(-(2.0 ** (-5.0 - h))) for h in range(N_HEADS_R)]

BF16 = jnp.bfloat16
F32 = jnp.float32

FFN_CHUNK = 256
FFN_TILE = 1024
FFN_SUBTILE = 512
FFN_MIN_PIECE = 256
MIX_TILE = 256
RET_CHUNK = 256
VMEM_LIMIT = 60 * 1024 * 1024


def _dot(a, b):
    return jnp.dot(a, b, preferred_element_type=F32)


def _dot_nt(a, b):
    return lax.dot_general(a, b, (((1,), (1,)), ((), ())), preferred_element_type=F32)


def _layer_norm(v, g, b):
    mu = jnp.mean(v, axis=-1, keepdims=True)
    d = v - mu
    var = jnp.mean(d * d, axis=-1, keepdims=True)
    return d * lax.rsqrt(var + LN_EPS) * g + b


def _silu(a):
    return a * jax.nn.sigmoid(a)


def _cast_side(refs, n_side):
    for k in range(n_side):
        refs[n_side + k][...] = refs[k][...].astype(BF16)


BF16_SUBLANES = 16


def _side(w, layer, total):
    r = w.shape[1]
    nblk = total
    while r % nblk or (r // nblk) % BF16_SUBLANES:
        nblk //= 2
    return (w, layer, r // nblk)


def _side_specs(side, step, total):
    in_specs, out_specs, out_shape, args = [], [], [], []
    for w, layer, rows in side:
        _, r, c = w.shape
        nblk = r // rows
        assert r % rows == 0 and total % nblk == 0
        idx = lambda *g, nblk=nblk: step(*g) // (total // nblk)
        in_specs.append(pl.BlockSpec((None, rows, c), lambda *g, idx=idx, layer=layer: (layer, idx(*g), 0)))
        out_specs.append(pl.BlockSpec((rows, c), lambda *g, idx=idx: (idx(*g), 0)))
        out_shape.append(jax.ShapeDtypeStruct((r, c), BF16))
        args.append(w)
    return in_specs, out_specs, out_shape, args


def _down_pieces(tm):
    pieces, left = [], tm
    while left > 2 * FFN_MIN_PIECE:
        pieces.append(left // 2)
        left -= left // 2
    return pieces + ([left // 2, left // 2] if left > FFN_MIN_PIECE else [left])


def _ffn_body(x_ref, xs_ref, wgu_ref, wdn_ref, g_ref, b_ref, *rest, n_side):
    side_in, rest = rest[:n_side], rest[n_side:]
    (o_ref, os_ref), rest = rest[:2], rest[2:]
    side_out, (h_ref,) = rest[:n_side], rest[n_side:]
    _cast_side(side_in + side_out, n_side)

    @pl.when(pl.program_id(0) == 0)
    def _():
        _ffn_rows(xs_ref, os_ref, h_ref, wgu_ref, wdn_ref, g_ref, b_ref)

    _ffn_rows(x_ref, o_ref, h_ref, wgu_ref, wdn_ref, g_ref, b_ref)


def _ffn_rows(x_ref, o_ref, h_ref, wgu_ref, wdn_ref, g_ref, b_ref):
    tm = x_ref.shape[0]
    sub = min(FFN_SUBTILE, tm)
    for s in range(tm // sub):
        rows = slice(s * sub, (s + 1) * sub)
        xb = x_ref[rows, :].astype(BF16)
        for j in range(D_FF // FFN_CHUNK):
            lo = j * FFN_CHUNK
            a = _dot(xb, wgu_ref[:, lo:lo + FFN_CHUNK])
            u = _dot(xb, wgu_ref[:, D_FF + lo:D_FF + lo + FFN_CHUNK])
            h_ref[rows, lo:lo + FFN_CHUNK] = (_silu(a) * u).astype(BF16)
    start = 0
    for size in _down_pieces(tm):
        rows = slice(start, start + size)
        d = _dot(h_ref[rows, :], wdn_ref[...])
        o_ref[rows, :] = _layer_norm(ALPHA * x_ref[rows, :] + 0.5 * d, g_ref[...], b_ref[...])
        start += size


def _resident(shape, layer):
    nd = len(shape)
    return pl.BlockSpec((None,) + shape, lambda *_: (layer,) + (0,) * nd,
                        pipeline_mode=pl.Buffered(1))


def _whole(a):
    return pl.BlockSpec(a.shape, lambda *_: (0,) * a.ndim, pipeline_mode=pl.Buffered(1))


def _ffn(x, xs, wgu, wdn, g, b, layer, tm, side=()):
    m = x.shape[0]
    n = xs.shape[0]
    steps = m // tm
    s_in, s_out, s_shape, s_args = _side_specs(side, lambda i: i, steps)
    outs = pl.pallas_call(
        functools.partial(_ffn_body, n_side=len(side)),
        grid=(steps,),
        in_specs=[
            pl.BlockSpec((tm, D_MODEL), lambda i: (i, 0)),
            pl.BlockSpec((n, D_MODEL), lambda i: (0, 0)),
            _whole(wgu),
            _whole(wdn),
            _resident((1, D_MODEL), layer),
            _resident((1, D_MODEL), layer),
        ] + s_in,
        out_specs=[pl.BlockSpec((tm, D_MODEL), lambda i: (i, 0)),
                   pl.BlockSpec((n, D_MODEL), lambda i: (0, 0))] + s_out,
        out_shape=[jax.ShapeDtypeStruct((m, D_MODEL), F32),
                   jax.ShapeDtypeStruct((n, D_MODEL), F32)] + s_shape,
        scratch_shapes=[pltpu.VMEM((tm, D_FF), BF16)],
        compiler_params=pltpu.CompilerParams(
            dimension_semantics=("arbitrary",), vmem_limit_bytes=VMEM_LIMIT),
        name="ffn",
    )(x, xs, wgu, wdn, g, b, *s_args)
    return outs[0], outs[1], outs[2:]


def _mix_prompt_body(sinks_ref, x_ref, win_ref, bias_ref, dec_ref, rowdec_ref, kdec_ref, gn_ref,
                     wa_ref, wr_ref, wo_ref, g_ref, b_ref, *rest, layer, n_side, n_alias):
    smp_in, rest = rest[:7], rest[7 + n_alias:]
    side_in, rest = rest[:n_side], rest[n_side:]
    (y_ref, kwin_ref, vwin_ref, state_ref), rest = rest[:4], rest[4:]
    smp_out, rest = rest[:5], rest[5:]
    side_out, rest = rest[:n_side], rest[n_side:]
    qt_s, vt0, vt1, vt2, vt3, k0, k1, k2, k3, oa_s, orf_s, p_s, inv_s, ga_s, gr_s = rest
    _mix_prompt_tile(sinks_ref, x_ref, win_ref, bias_ref, dec_ref, rowdec_ref, kdec_ref, gn_ref,
                     wa_ref, wr_ref, wo_ref, g_ref, b_ref, y_ref, kwin_ref, vwin_ref, state_ref,
                     qt_s, vt0, vt1, vt2, vt3, k0, k1, k2, k3, oa_s, orf_s, p_s, inv_s, ga_s, gr_s, layer=layer)
    _cast_side(side_in + side_out, n_side)
    tile = pl.program_id(0) * pl.num_programs(1) + pl.program_id(1)
    _smp_core_tile(sinks_ref, *smp_in, *smp_out, layer=layer, tile=tile)


def _mix_prompt_tile(sinks_ref, x_ref, win_ref, bias_ref, dec_ref, rowdec_ref, kdec_ref, gn_ref,
                     wa_ref, wr_ref, wo_ref, g_ref, b_ref,
                     y_ref, kwin_ref, vwin_ref, state_ref,
                     qt_s, vt0, vt1, vt2, vt3, k0, k1, k2, k3, oa_s, orf_s, p_s, inv_s, ga_s, gr_s,
                     *, layer):
    t = pl.program_id(1)
    T = MIX_TILE
    NB = T // BLOCK
    kbufs = (k0, k1, k2, k3)
    vtbufs = (vt0, vt1, vt2, vt3)

    @pl.when(t == 0)
    def _():
        state_ref[...] = jnp.zeros_like(state_ref)
        for buf in vtbufs:
            buf[:, 0:BLOCK] = jnp.zeros((KA, BLOCK), BF16)
        for buf in kbufs:
            buf[0:BLOCK, :] = jnp.zeros((BLOCK, KA), BF16)

    x = x_ref[...]
    xb = x.astype(BF16)

    za = _dot(xb, win_ref[:, C_QA:C_QR])
    q = za[:, 0:QA] * (HEAD_DIM_A ** -0.5)
    for c in range(NB):
        for p in range(N_HEADS_A // 2):
            col = c * 2 * BLOCK + (p % 2) * BLOCK
            qt_s[p // 2, :, col:col + BLOCK] = (
                q[c * BLOCK:(c + 1) * BLOCK, p * KA:(p + 1) * KA].T.astype(BF16))
    ka = za[:, QA:QA + KA]
    va = za[:, QA + KA:QA + 2 * KA]
    kwin_ref[...] = ka[T - WINDOW:, :]
    vwin_ref[...] = va[T - WINDOW:, :]
    ka_r = pltpu.roll(ka, HEAD_DIM_A, 1)
    low = lax.broadcasted_iota(jnp.int32, (T, KA), 1) < HEAD_DIM_A
    k0[BLOCK:, :] = jnp.where(low, ka, 0.0).astype(BF16)
    k1[BLOCK:, :] = jnp.where(low, 0.0, ka_r).astype(BF16)
    k2[BLOCK:, :] = jnp.where(low, ka_r, 0.0).astype(BF16)
    k3[BLOCK:, :] = jnp.where(low, 0.0, ka).astype(BF16)
    va_t = va.T
    va_tr = pltpu.roll(va_t, HEAD_DIM_A, 0)
    top = lax.broadcasted_iota(jnp.int32, (KA, T), 0) < HEAD_DIM_A
    vt0[:, BLOCK:] = jnp.where(top, va_t, 0.0).astype(BF16)
    vt1[:, BLOCK:] = jnp.where(top, 0.0, va_tr).astype(BF16)
    vt2[:, BLOCK:] = jnp.where(top, va_tr, 0.0).astype(BF16)
    vt3[:, BLOCK:] = jnp.where(top, 0.0, va_t).astype(BF16)

    first_slab = lax.broadcasted_iota(jnp.int32, (1, 2 * BLOCK), 1) < BLOCK
    sink_row = []
    for g in range(N_KV_A):
        for e in range(2):
            sink_row.append(jnp.where(first_slab, sinks_ref[layer, 4 * g + e], sinks_ref[layer, 4 * g + 2 + e]))
    top_o = lax.broadcasted_iota(jnp.int32, (KA, 2 * BLOCK), 0) < HEAD_DIM_A

    def scores(c):
        out = []
        for g in range(N_KV_A):
            qg_t = qt_s[g, :, c * 2 * BLOCK:(c + 1) * 2 * BLOCK]
            for e in range(2):
                out.append(_dot(kbufs[2 * g + e][c * BLOCK:(c + 2) * BLOCK, :], qg_t))
        return out

    def softmax(c, s_list):
        for ge in range(4):
            if c == 0:
                bias = bias_ref[jnp.where(t == 0, 4, 0) + ge]
            else:
                bias = bias_ref[ge]
            s = s_list[ge] + bias
            m = jnp.maximum(jnp.max(s, axis=0, keepdims=True), sink_row[ge])
            pe = jnp.exp(s - m)
            den = jnp.sum(pe, axis=0, keepdims=True) + jnp.exp(sink_row[ge] - m)
            p_s[c * 4 + ge] = pe.astype(BF16)
            inv_s[c * 4 + ge] = jnp.broadcast_to(1.0 / den, (8, 2 * BLOCK))

    def attend(c):
        for g in range(N_KV_A):
            o_t = (_dot(vtbufs[2 * g][:, c * BLOCK:(c + 2) * BLOCK], p_s[c * 4 + 2 * g])
                   + _dot(vtbufs[2 * g + 1][:, c * BLOCK:(c + 2) * BLOCK], p_s[c * 4 + 2 * g + 1]))
            o_t = o_t * jnp.where(top_o, inv_s[c * 4 + 2 * g][0:1, :], inv_s[c * 4 + 2 * g + 1][0:1, :])
            oa_s[c * BLOCK:(c + 1) * BLOCK, 2 * g * KA:(2 * g + 1) * KA] = o_t[:, 0:BLOCK].T.astype(BF16)
            oa_s[c * BLOCK:(c + 1) * BLOCK, (2 * g + 1) * KA:(2 * g + 2) * KA] = o_t[:, BLOCK:].T.astype(BF16)

    RC = RET_CHUNK
    NR = T // RC
    g_chunk = [math.exp(RC * lg) for lg in LOG_GAMMA]

    def ret_local(zr, vr, cc):
        sc, upd = [], []
        for h in range(N_HEADS_R):
            qh = zr[cc * RC:(cc + 1) * RC, DK_R * h:DK_R * (h + 1)].astype(BF16)
            kf = zr[cc * RC:(cc + 1) * RC, QR + DK_R * h:QR + DK_R * (h + 1)] * (DK_R ** -0.5)
            vh = vr[cc * RC:(cc + 1) * RC, DV_R * h:DV_R * (h + 1)]
            sc.append(_dot_nt(qh, kf.astype(BF16)))
            upd.append(_dot((kf * kdec_ref[h]).T.astype(BF16), vh))
        return sc, upd

    def ret_out(zr, vr, gr, cc, sc, s_in):
        for h in range(N_HEADS_R):
            qh = zr[cc * RC:(cc + 1) * RC, DK_R * h:DK_R * (h + 1)].astype(BF16)
            vh = vr[cc * RC:(cc + 1) * RC, DV_R * h:DV_R * (h + 1)]
            o = (_dot((sc[h] * dec_ref[h]).astype(BF16), vh)
                 + _dot(qh, s_in[h].astype(BF16)) * rowdec_ref[h])
            mu = jnp.mean(o, axis=-1, keepdims=True)
            d = o - mu
            var = jnp.mean(d * d, axis=-1, keepdims=True)
            on = d * lax.rsqrt(var + GN_EPS)
            gate = _silu(gr[cc * RC:(cc + 1) * RC, DV_R * h:DV_R * (h + 1)])
            orf_s[cc * RC:(cc + 1) * RC, DV_R * h:DV_R * (h + 1)] = (
                on * gn_ref[:, DV_R * h:DV_R * (h + 1)] * gate).astype(BF16)

    s_all = [scores(c) for c in range(NB)]
    ga_s[...] = jax.nn.sigmoid(_dot(xb, win_ref[:, C_GATE:C_GATE + D_MODEL]))
    for c in range(NB):
        softmax(c, s_all[c])
    gr_s[...] = jax.nn.sigmoid(_dot(xb, win_ref[:, C_GATE + D_MODEL:IN_COLS]))
    zr = _dot(xb, win_ref[:, C_QR:C_VR])
    vr = _dot(xb, win_ref[:, C_VR:C_GR]).astype(BF16)
    gr = _dot(xb, win_ref[:, C_GR:C_GATE])
    for c in range(NB):
        attend(c)
    for buf in vtbufs:
        buf[:, 0:BLOCK] = buf[:, T:T + BLOCK]
    for buf in kbufs:
        buf[0:BLOCK, :] = buf[T:T + BLOCK, :]

    local = [ret_local(zr, vr, cc) for cc in range(NR)]
    states = [[state_ref[h] for h in range(N_HEADS_R)]]
    for cc in range(NR):
        states.append([g_chunk[h] * states[cc][h] + local[cc][1][h] for h in range(N_HEADS_R)])
    for h in range(N_HEADS_R):
        state_ref[h] = states[NR][h]
    ya = _dot(oa_s[...], wa_ref[...])
    for cc in range(NR):
        rows = slice(cc * RC, (cc + 1) * RC)
        ret_out(zr, vr, gr, cc, local[cc][0], states[cc])
        merged = ga_s[rows, :] * ya[rows, :] + gr_s[rows, :] * _dot(orf_s[rows, :], wr_ref[...])
        y = _dot(merged.astype(BF16), wo_ref[...])
        y_ref[rows, :] = _layer_norm(ALPHA * x[rows, :] + y, g_ref[...], b_ref[...])


def _attention_bias():
    i = np.arange(BLOCK)[:, None]
    j = np.arange(2 * BLOCK)[None, :]
    dist = i + BLOCK - j
    band = (dist >= 0) & (dist <= WINDOW)
    out = np.zeros((8, 2 * BLOCK, 2 * BLOCK), np.float32)
    for first in range(2):
        valid = band & ((j >= BLOCK) | (first == 0))
        for g in range(N_KV_A):
            for e in range(2):
                for half in range(2):
                    h = 4 * g + 2 * half + e
                    out[4 * first + 2 * g + e, :, half * BLOCK:(half + 1) * BLOCK] = np.where(
                        valid, -SLOPES[h] * dist, NEG).T
    return jnp.asarray(out)


def _retention_tables():
    pos = np.arange(RET_CHUNK, dtype=np.float64)
    diff = pos[:, None] - pos[None, :]
    lg = np.asarray(LOG_GAMMA)[:, None, None]
    dec = np.where(diff >= 0, np.exp(np.maximum(diff, 0.0) * lg), 0.0)
    rowdec = np.broadcast_to(np.exp((pos + 1.0)[None, :, None] * lg), (N_HEADS_R, RET_CHUNK, DV_R))
    kdec = np.broadcast_to(np.exp((RET_CHUNK - 1.0 - pos)[None, :, None] * lg),
                           (N_HEADS_R, RET_CHUNK, DK_R))
    return (jnp.asarray(dec, F32), jnp.asarray(rowdec, F32), jnp.asarray(kdec, F32))


def _mix_prompt(x, sinks, w_in, gn, wa, wr, wo, g, b, layer, smp, stacked, side=()):
    B, S, _ = x.shape
    T = MIX_TILE
    nt = S // T
    bias = _attention_bias()
    dec, rowdec, kdec = _retention_tables()
    step = lambda bi, ti: bi * nt + ti
    s_in, s_out, s_shape, s_args = _side_specs(side, step, B * nt)
    n = smp[0].shape[0]
    c_in, c_args, n_alias, c_out, c_shape = _smp_core_specs(*smp, stacked, layer, n // (B * nt), step)
    n_fixed = 13
    aliases = {n_fixed + 7 + k: 4 + 2 + k for k in range(n_alias)}
    outs = pl.pallas_call(
        functools.partial(_mix_prompt_body, layer=layer, n_side=len(side), n_alias=n_alias),
        grid=(B, nt),
        in_specs=[
            pl.BlockSpec(memory_space=pltpu.SMEM),
            pl.BlockSpec((None, T, D_MODEL), lambda bi, ti: (bi, ti, 0)),
            _whole(w_in),
            _whole(bias), _whole(dec), _whole(rowdec), _whole(kdec),
            _resident((1, VR), layer),
            _whole(wa),
            _whole(wr),
            _whole(wo),
            _resident((1, D_MODEL), layer),
            _resident((1, D_MODEL), layer),
        ] + c_in + s_in,
        out_specs=[
            pl.BlockSpec((None, T, D_MODEL), lambda bi, ti: (bi, ti, 0)),
            pl.BlockSpec((None, WINDOW, KA), lambda bi, ti: (bi, 0, 0)),
            pl.BlockSpec((None, WINDOW, KA), lambda bi, ti: (bi, 0, 0)),
            pl.BlockSpec((None, N_HEADS_R, DK_R, DV_R), lambda bi, ti: (bi, 0, 0, 0)),
        ] + c_out + s_out,
        out_shape=[
            jax.ShapeDtypeStruct((B, S, D_MODEL), F32),
            jax.ShapeDtypeStruct((B, WINDOW, KA), F32),
            jax.ShapeDtypeStruct((B, WINDOW, KA), F32),
            jax.ShapeDtypeStruct((B, N_HEADS_R, DK_R, DV_R), F32),
        ] + c_shape + s_shape,
        input_output_aliases=aliases,
        scratch_shapes=(
            [pltpu.VMEM((N_KV_A, KA, T * 2), BF16)]
            + [pltpu.VMEM((KA, T + BLOCK), BF16) for _ in range(4)]
            + [pltpu.VMEM((T + BLOCK, KA), BF16) for _ in range(4)]
            + [pltpu.VMEM((T, QA), BF16), pltpu.VMEM((T, VR), BF16),
               pltpu.VMEM((T // BLOCK * 4, 2 * BLOCK, 2 * BLOCK), BF16),
               pltpu.VMEM((T // BLOCK * 4, 8, 2 * BLOCK), F32),
               pltpu.VMEM((T, D_MODEL), F32), pltpu.VMEM((T, D_MODEL), F32)]),
        compiler_params=pltpu.CompilerParams(
            dimension_semantics=("arbitrary", "arbitrary"), vmem_limit_bytes=VMEM_LIMIT),
        name="mix_prompt",
    )(sinks, x, w_in, bias, dec, rowdec, kdec, gn, wa, wr, wo, g, b, *c_args, *s_args)
    return outs[0], outs[1], outs[2], outs[3], outs[4:9], outs[9:]


def _head_placement():
    e = np.zeros((QA, N_HEADS_A * KA), np.float32)
    for h in range(N_HEADS_A):
        g = h // GROUP_A
        for d in range(HEAD_DIM_A):
            e[h * HEAD_DIM_A + d, h * KA + g * HEAD_DIM_A + d] = 1.0
    return e


def _smp_proj_body(x_ref, win_ref, e_ref, qm_ref, kvn_ref, zt_ref, vr_ref, zg_ref):
    z = _dot(x_ref[...].astype(BF16), win_ref[...])
    qa = (z[:, 0:QA] * (HEAD_DIM_A ** -0.5)).astype(BF16)
    qm_ref[...] = _dot(qa, e_ref[...])
    kvn_ref[...] = z[:, C_KA:C_QR]
    zt_ref[...] = z[:, C_QR:C_VR].T
    vr_ref[...] = z[:, C_VR:C_GR]
    zg_ref[...] = z[:, C_GR:]


def _smp_proj(x, w_in, e, layer):
    n = x.shape[0]
    widths = [(n, N_HEADS_A * KA), (n, 2 * KA), (2 * QR, n), (n, VR), (n, IN_COLS - C_GR)]
    return pl.pallas_call(
        _smp_proj_body,
        grid=(1,),
        in_specs=[
            pl.BlockSpec((n, D_MODEL), lambda i: (0, 0)),
            _whole(w_in),
            pl.BlockSpec((QA, N_HEADS_A * KA), lambda i: (0, 0)),
        ],
        out_specs=[pl.BlockSpec(s, lambda i: (0, 0)) for s in widths],
        out_shape=[jax.ShapeDtypeStruct(s, F32) for s in widths],
        compiler_params=pltpu.CompilerParams(
            dimension_semantics=("arbitrary",), vmem_limit_bytes=VMEM_LIMIT),
        name="smp_proj",
    )(x, w_in, e)


def _smp_core_tile(sinks_ref, qm_ref, kvn_ref, ck_ref, cv_ref, zt_ref, vr_ref, st_ref,
                   o3_ref, or_ref, nk_ref, nv_ref, ns_ref, *, layer, tile):
    tb = qm_ref.shape[0]
    row8 = lax.broadcasted_iota(jnp.int32, (N_HEADS_A, KA), 0)
    lane8 = lax.broadcasted_iota(jnp.int32, (N_HEADS_A, KA), 1)
    slope = jnp.zeros((N_HEADS_A, KA), F32)
    sinkv = jnp.zeros((N_HEADS_A, KA), F32)
    for h in range(N_HEADS_A):
        slope = jnp.where(row8 == h, SLOPES[h], slope)
        sinkv = jnp.where(row8 == h, sinks_ref[layer, h], sinkv)
    sink = sinkv[:, 0:1]
    bias = slope * (WINDOW - lane8).astype(F32)
    in_group = (lane8 < HEAD_DIM_A) == (row8 < GROUP_A)
    last_row = lax.broadcasted_iota(jnp.int32, (WINDOW, KA), 0) == WINDOW - 1
    n = zt_ref.shape[1]
    zt = pltpu.roll(zt_ref[...], lax.rem(n - tile * tb, n), 1)
    scores = [_dot_nt(qm_ref[bl].astype(BF16), ck_ref[bl].astype(BF16)) for bl in range(tb)]
    probs = []
    for bl in range(tb):
        qm = qm_ref[bl]
        kn = kvn_ref[bl:bl + 1, 0:KA]
        s = scores[bl] - bias
        s_new = jnp.sum(qm * kn, axis=-1, keepdims=True)
        m = jnp.maximum(jnp.maximum(jnp.max(s, axis=-1, keepdims=True), s_new), sink)
        p = jnp.exp(s - m)
        p_new = jnp.exp(s_new - m)
        inv = 1.0 / (jnp.sum(p, axis=-1, keepdims=True) + p_new + jnp.exp(sink - m))
        probs.append(((p * inv).astype(BF16), p_new * inv))
    for bl in range(tb):
        kb = ck_ref[bl]
        vb = cv_ref[bl]
        kn = kvn_ref[bl:bl + 1, 0:KA]
        vn = kvn_ref[bl:bl + 1, KA:2 * KA]
        o = _dot(probs[bl][0], vb.astype(BF16)) + probs[bl][1] * vn
        o3_ref[bl] = jnp.where(in_group, o, 0.0)
        nk_ref[bl] = jnp.where(last_row, kn, pltpu.roll(kb, WINDOW - 1, 0))
        nv_ref[bl] = jnp.where(last_row, vn, pltpu.roll(vb, WINDOW - 1, 0))
    for bl in range(tb):
        for h in range(N_HEADS_R):
            gamma = math.exp(LOG_GAMMA[h])
            qc = zt[DK_R * h:DK_R * (h + 1), bl:bl + 1]
            kc = zt[QR + DK_R * h:QR + DK_R * (h + 1), bl:bl + 1] * (DK_R ** -0.5)
            s0 = st_ref[bl, h]
            v = vr_ref[bl:bl + 1, DV_R * h:DV_R * (h + 1)]
            sc = jnp.sum(qc * kc, axis=0, keepdims=True)
            o_r = sc * v + jnp.sum(qc * s0, axis=0, keepdims=True) * gamma
            ns_ref[bl, h] = gamma * s0 + kc * v
            or_ref[bl:bl + 1, DV_R * h:DV_R * (h + 1)] = o_r


def _smp_core_specs(qm3, kvn, ck, cv, zt, vr, st, stacked, layer, tb, step):
    n = qm3.shape[0]
    nt = n // tb
    tile3 = lambda *g: (step(*g), 0, 0)
    lay4 = lambda *g: (layer, step(*g), 0, 0)
    lay5 = lambda *g: (layer, step(*g), 0, 0, 0)
    in_specs = [
        pl.BlockSpec((tb, N_HEADS_A, KA), tile3),
        pl.BlockSpec((None, tb, 2 * KA), tile3),
        pl.BlockSpec((None, tb, WINDOW, KA), lay4),
        pl.BlockSpec((None, tb, WINDOW, KA), lay4),
        pl.BlockSpec((2 * QR, n), lambda *g: (0, 0)),
        pl.BlockSpec((None, tb, VR), tile3),
        pl.BlockSpec((None, tb, N_HEADS_R, DK_R, DV_R), lay5),
    ]
    args = [qm3, kvn.reshape(nt, tb, 2 * KA), ck, cv,
            zt, vr.reshape(nt, tb, VR), st]
    n_alias = 0
    if stacked is not None:
        n_alias = len(stacked)
        in_specs += [pl.BlockSpec(memory_space=pl.ANY)] * n_alias
        args += list(stacked)
    out_specs = [
        pl.BlockSpec((tb, N_HEADS_A, KA), tile3),
        pl.BlockSpec((None, tb, VR), tile3),
        pl.BlockSpec((None, tb, WINDOW, KA), lay4),
        pl.BlockSpec((None, tb, WINDOW, KA), lay4),
        pl.BlockSpec((None, tb, N_HEADS_R, DK_R, DV_R), lay5),
    ]
    out_shape = [
        jax.ShapeDtypeStruct((n, N_HEADS_A, KA), F32),
        jax.ShapeDtypeStruct((nt, tb, VR), F32),
        jax.ShapeDtypeStruct((DEPTH, n, WINDOW, KA), F32),
        jax.ShapeDtypeStruct((DEPTH, n, WINDOW, KA), F32),
        jax.ShapeDtypeStruct((DEPTH, n, N_HEADS_R, DK_R, DV_R), F32),
    ]
    return in_specs, args, n_alias, out_specs, out_shape


def _smp_out_body(x_ref, zg_ref, o3_ref, or_ref, et_ref, gn_ref, wa_ref, wr_ref, wo_ref,
                  g_ref, b_ref, y_ref):
    x = x_ref[...]
    oa = _dot(o3_ref[...].astype(BF16), et_ref[...]).astype(BF16)
    gr = zg_ref[:, 0:VR]
    parts = []
    for h in range(N_HEADS_R):
        o = or_ref[:, DV_R * h:DV_R * (h + 1)]
        mu = jnp.mean(o, axis=-1, keepdims=True)
        d = o - mu
        var = jnp.mean(d * d, axis=-1, keepdims=True)
        parts.append(d * lax.rsqrt(var + GN_EPS))
    orn = jnp.concatenate(parts, axis=-1)
    orf = (orn * gn_ref[...] * _silu(gr)).astype(BF16)
    merged = (jax.nn.sigmoid(zg_ref[:, VR:VR + D_MODEL]) * _dot(oa, wa_ref[...])
              + jax.nn.sigmoid(zg_ref[:, VR + D_MODEL:]) * _dot(orf, wr_ref[...]))
    y = _dot(merged.astype(BF16), wo_ref[...])
    y_ref[...] = _layer_norm(ALPHA * x + y, g_ref[...], b_ref[...])


def _smp_out(x, zg, o3, o_r, et, gn, wa, wr, wo, g, b, layer):
    n = x.shape[0]
    whole = lambda a: pl.BlockSpec(a.shape, lambda i: (0,) * a.ndim)
    return pl.pallas_call(
        _smp_out_body,
        grid=(1,),
        in_specs=[
            whole(x), whole(zg), whole(o3), whole(o_r), whole(et),
            _resident((1, VR), layer),
            _whole(wa),
            _whole(wr),
            _whole(wo),
            _resident((1, D_MODEL), layer),
            _resident((1, D_MODEL), layer),
        ],
        out_specs=pl.BlockSpec((n, D_MODEL), lambda i: (0, 0)),
        out_shape=jax.ShapeDtypeStruct((n, D_MODEL), F32),
        compiler_params=pltpu.CompilerParams(
            dimension_semantics=("arbitrary",), vmem_limit_bytes=VMEM_LIMIT),
        name="smp_out",
    )(x, zg, o3, o_r, et, gn, wa, wr, wo, g, b)


def kernel(x_prompt, x_sample, cache_win_k, cache_win_v, state_ret, w_ff1_gu, w_ff1_dn, ln1_g, ln1_b, w_in, attn_sinks, ret_gn_g, w_br_a, w_br_r, w_o, ln2_g, ln2_b, w_ff2_gu, w_ff2_dn, ln3_g, ln3_b):
    B, S, _ = x_prompt.shape
    n = x_sample.shape[0]
    row = lambda p: p.reshape(DEPTH, 1, -1)
    g1, b1, g2, b2, g3, b3 = row(ln1_g), row(ln1_b), row(ln2_g), row(ln2_b), row(ln3_g), row(ln3_b)
    gn = row(ret_gn_g)
    e_np = _head_placement()
    e = jnp.asarray(e_np, BF16)
    et = jnp.asarray(e_np.T, BF16)
    ck = cache_win_k.reshape(DEPTH, n, WINDOW, KA)
    cv = cache_win_v.reshape(DEPTH, n, WINDOW, KA)

    xp = x_prompt.reshape(B * S, D_MODEL)
    xs = x_sample.reshape(n, D_MODEL)
    pk, pv, pr = [], [], []
    stacked = None
    ffn_steps = B * S // FFN_TILE
    mix_steps = B * S // MIX_TILE
    ffn1_w = (w_ff1_gu[0].astype(BF16), w_ff1_dn[0].astype(BF16))
    for l in range(DEPTH):
        xp, xs, (win, wa, wr, wo) = _ffn(
            xp, xs, *ffn1_w, g1, b1, l, FFN_TILE,
            side=[_side(w, l, ffn_steps) for w in (w_in, w_br_a, w_br_r, w_o)])
        qm, kvn, zt, vr, zg = _smp_proj(xs, win, e, l)
        smp = (qm.reshape(n, N_HEADS_A, KA), kvn, ck, cv, zt, vr, state_ret)
        yp, kw, vw, st, (o3, o_r, *stacked), ffn2_w = _mix_prompt(
            xp.reshape(B, S, D_MODEL), attn_sinks, win, gn, wa, wr, wo, g2, b2, l, smp, stacked,
            side=[_side(w, l, mix_steps) for w in (w_ff2_gu, w_ff2_dn)])
        xp = yp.reshape(B * S, D_MODEL)
        pk.append(kw)
        pv.append(vw)
        pr.append(st)
        xs = _smp_out(xs, zg, o3.reshape(n, N_HEADS_A * KA), o_r.reshape(n, VR), et, gn, wa, wr, wo,
                      g2, b2, l)
        nxt = [_side(w, l + 1, ffn_steps) for w in (w_ff1_gu, w_ff1_dn)] if l + 1 < DEPTH else []
        xp, xs, ffn1_w = _ffn(xp, xs, *ffn2_w, g3, b3, l, FFN_TILE, side=nxt)
    win_shape = (DEPTH, -1, WINDOW, N_KV_A, HEAD_DIM_A)
    return (xp.reshape(B, S, D_MODEL), xs.reshape(n, 1, D_MODEL),
            jnp.stack(pk).reshape(win_shape), jnp.stack(pv).reshape(win_shape), jnp.stack(pr),
            stacked[0].reshape(win_shape), stacked[1].reshape(win_shape), stacked[2])
```

```python
import functools
import math

import numpy as np
import jax
import jax.numpy as jnp
from jax import lax
from jax.experimental import pallas as pl
from jax.experimental.pallas import tpu as pltpu

D_MODEL = 1024
DEPTH = 4
N_HEADS_A = 8
N_KV_A = 2
HEAD_DIM_A = 64
GROUP_A = N_HEADS_A // N_KV_A
WINDOW = 128
BLOCK = 128
N_HEADS_R = 4
DK_R = 128
DV_R = 256
D_FF = 2816
ALPHA = (2.0 * DEPTH) ** 0.25
LN_EPS = 1e-5
GN_EPS = 1e-6
NEG = -1e30

QA = N_HEADS_A * HEAD_DIM_A
KA = N_KV_A * HEAD_DIM_A
QR = N_HEADS_R * DK_R
VR = N_HEADS_R * DV_R
C_QA, C_KA, C_VA = 0, QA, QA + KA
C_QR = QA + 2 * KA
C_KR = C_QR + QR
C_VR = C_KR + QR
C_GR = C_VR + VR
C_GATE = C_GR + VR
IN_COLS = C_GATE + 2 * D_MODEL

SLOPES = [2.0 ** (-8.0 * (h + 1) / N_HEADS_A) for h in range(N_HEADS_A)]
LOG_GAMMA = [math.log1p(-(2.0 ** (-5.0 - h))) for h in range(N_HEADS_R)]

BF16 = jnp.bfloat16
F32 = jnp.float32

FFN_CHUNK = 256
FFN_TILE = 1024
FFN_SUBTILE = 512
FFN_MIN_PIECE = 256
MIX_TILE = 256
RET_CHUNK = 256
VMEM_LIMIT = 60 * 1024 * 1024
PARAM_ROWS = 8


def _dot(a, b):
    return jnp.dot(a, b, preferred_element_type=F32)


def _dot_nt(a, b):
    return lax.dot_general(a, b, (((1,), (1,)), ((), ())), preferred_element_type=F32)


def _layer_norm(v, g, b):
    mu = jnp.mean(v, axis=-1, keepdims=True)
    d = v - mu
    var = jnp.mean(d * d, axis=-1, keepdims=True)
    return d * lax.rsqrt(var + LN_EPS) * g + b


def _silu(a):
    return a * jax.nn.sigmoid(a)


def _cast_side(refs, n_side):
    for k in range(n_side):
        refs[n_side + k][...] = refs[k][...].astype(BF16)


BF16_SUBLANES = 16


def _side(w, layer, total):
    r = w.shape[1]
    nblk = total
    while r % nblk or (r // nblk) % BF16_SUBLANES:
        nblk //= 2
    return (w, layer, r // nblk)


def _side_specs(side, step, total):
    in_specs, out_specs, out_shape, args = [], [], [], []
    for w, layer, rows in side:
        _, r, c = w.shape
        nblk = r // rows
        assert r % rows == 0 and total % nblk == 0
        idx = lambda *g, nblk=nblk: step(*g) // (total // nblk)
        in_specs.append(pl.BlockSpec((None, rows, c), lambda *g, idx=idx, layer=layer: (layer, idx(*g), 0)))
        out_specs.append(pl.BlockSpec((rows, c), lambda *g, idx=idx: (idx(*g), 0)))
        out_shape.append(jax.ShapeDtypeStruct((r, c), BF16))
        args.append(w)
    return in_specs, out_specs, out_shape, args


def _down_pieces(tm):
    pieces, left = [], tm
    while left > 2 * FFN_MIN_PIECE:
        pieces.append(left // 2)
        left -= left // 2
    return pieces + ([left // 2, left // 2] if left > FFN_MIN_PIECE else [left])


def _ffn_body(x_ref, xs_ref, wgu_ref, wdn_ref, g_ref, b_ref, *rest, n_side):
    side_in, rest = rest[:n_side], rest[n_side:]
    (o_ref, os_ref), rest = rest[:2], rest[2:]
    side_out, (h_ref,) = rest[:n_side], rest[n_side:]
    _cast_side(side_in + side_out, n_side)

    @pl.when(pl.program_id(0) == 0)
    def _():
        _ffn_rows(xs_ref, os_ref, h_ref, wgu_ref, wdn_ref, g_ref, b_ref)

    _ffn_rows(x_ref, o_ref, h_ref, wgu_ref, wdn_ref, g_ref, b_ref)


def _ffn_rows(x_ref, o_ref, h_ref, wgu_ref, wdn_ref, g_ref, b_ref):
    tm = x_ref.shape[0]
    sub = min(FFN_SUBTILE, tm)
    for s in range(tm // sub):
        rows = slice(s * sub, (s + 1) * sub)
        xb = x_ref[rows, :].astype(BF16)
        for j in range(D_FF // FFN_CHUNK):
            lo = j * FFN_CHUNK
            a = _dot(xb, wgu_ref[:, lo:lo + FFN_CHUNK])
            u = _dot(xb, wgu_ref[:, D_FF + lo:D_FF + lo + FFN_CHUNK])
            h_ref[rows, lo:lo + FFN_CHUNK] = (_silu(a) * u).astype(BF16)
    start = 0
    for size in _down_pieces(tm):
        rows = slice(start, start + size)
        d = _dot(h_ref[rows, :], wdn_ref[...])
        o_ref[rows, :] = _layer_norm(ALPHA * x_ref[rows, :] + 0.5 * d, g_ref[0:1, :], b_ref[0:1, :])
        start += size


def _resident(shape, layer):
    nd = len(shape)
    return pl.BlockSpec((None,) + shape, lambda *_: (layer,) + (0,) * nd,
                        pipeline_mode=pl.Buffered(1))


def _whole(a):
    return pl.BlockSpec(a.shape, lambda *_: (0,) * a.ndim, pipeline_mode=pl.Buffered(1))


def _ffn(x, xs, wgu, wdn, g, b, layer, tm, side=()):
    m = x.shape[0]
    n = xs.shape[0]
    steps = m // tm
    s_in, s_out, s_shape, s_args = _side_specs(side, lambda i: i, steps)
    outs = pl.pallas_call(
        functools.partial(_ffn_body, n_side=len(side)),
        grid=(steps,),
        in_specs=[
            pl.BlockSpec((tm, D_MODEL), lambda i: (i, 0)),
            pl.BlockSpec((n, D_MODEL), lambda i: (0, 0)),
            _whole(wgu),
            _whole(wdn),
            _resident((PARAM_ROWS,D_MODEL), layer),
            _resident((PARAM_ROWS,D_MODEL), layer),
        ] + s_in,
        out_specs=[pl.BlockSpec((tm, D_MODEL), lambda i: (i, 0)),
                   pl.BlockSpec((n, D_MODEL), lambda i: (0, 0))] + s_out,
        out_shape=[jax.ShapeDtypeStruct((m, D_MODEL), F32),
                   jax.ShapeDtypeStruct((n, D_MODEL), F32)] + s_shape,
        scratch_shapes=[pltpu.VMEM((tm, D_FF), BF16)],
        compiler_params=pltpu.CompilerParams(
            dimension_semantics=("arbitrary",), vmem_limit_bytes=VMEM_LIMIT),
        name="ffn",
    )(x, xs, wgu, wdn, g, b, *s_args)
    return outs[0], outs[1], outs[2:]


def _mix_prompt_body(sinks_ref, x_ref, win_ref, bias_ref, dec_ref, rowdec_ref, kdec_ref, gn_ref,
                     wa_ref, wr_ref, wo_ref, g_ref, b_ref, *rest, layer, n_side, n_alias):
    smp_in, rest = rest[:7], rest[7 + n_alias:]
    side_in, rest = rest[:n_side], rest[n_side:]
    (y_ref, kwin_ref, vwin_ref, state_ref), rest = rest[:4], rest[4:]
    smp_out, rest = rest[:5], rest[5:]
    side_out, rest = rest[:n_side], rest[n_side:]
    qt_s, vt0, vt1, vt2, vt3, k0, k1, k2, k3, oa_s, orf_s, p_s, inv_s, ga_s, gr_s = rest
    _mix_prompt_tile(sinks_ref, x_ref, win_ref, bias_ref, dec_ref, rowdec_ref, kdec_ref, gn_ref,
                     wa_ref, wr_ref, wo_ref, g_ref, b_ref, y_ref, kwin_ref, vwin_ref, state_ref,
                     qt_s, vt0, vt1, vt2, vt3, k0, k1, k2, k3, oa_s, orf_s, p_s, inv_s, ga_s, gr_s, layer=layer)
    _cast_side(side_in + side_out, n_side)
    tile = pl.program_id(0) * pl.num_programs(1) + pl.program_id(1)
    _smp_core_tile(sinks_ref, *smp_in, *smp_out, layer=layer, tile=tile)


def _mix_prompt_tile(sinks_ref, x_ref, win_ref, bias_ref, dec_ref, rowdec_ref, kdec_ref, gn_ref,
                     wa_ref, wr_ref, wo_ref, g_ref, b_ref,
                     y_ref, kwin_ref, vwin_ref, state_ref,
                     qt_s, vt0, vt1, vt2, vt3, k0, k1, k2, k3, oa_s, orf_s, p_s, inv_s, ga_s, gr_s,
                     *, layer):
    t = pl.program_id(1)
    T = MIX_TILE
    NB = T // BLOCK
    kbufs = (k0, k1, k2, k3)
    vtbufs = (vt0, vt1, vt2, vt3)

    @pl.when(t == 0)
    def _():
        state_ref[...] = jnp.zeros_like(state_ref)
        for buf in vtbufs:
            buf[:, 0:BLOCK] = jnp.zeros((KA, BLOCK), BF16)
        for buf in kbufs:
            buf[0:BLOCK, :] = jnp.zeros((BLOCK, KA), BF16)

    x = x_ref[...]
    xb = x.astype(BF16)

    za = _dot(xb, win_ref[:, C_QA:C_QR])
    q = za[:, 0:QA] * (HEAD_DIM_A ** -0.5)
    for c in range(NB):
        for p in range(N_HEADS_A // 2):
            col = c * 2 * BLOCK + (p % 2) * BLOCK
            qt_s[p // 2, :, col:col + BLOCK] = (
                q[c * BLOCK:(c + 1) * BLOCK, p * KA:(p + 1) * KA].T.astype(BF16))
    ka = za[:, QA:QA + KA]
    va = za[:, QA + KA:QA + 2 * KA]
    kwin_ref[...] = ka[T - WINDOW:, :]
    vwin_ref[...] = va[T - WINDOW:, :]
    ka_r = pltpu.roll(ka, HEAD_DIM_A, 1)
    low = lax.broadcasted_iota(jnp.int32, (T, KA), 1) < HEAD_DIM_A
    k0[BLOCK:, :] = jnp.where(low, ka, 0.0).astype(BF16)
    k1[BLOCK:, :] = jnp.where(low, 0.0, ka_r).astype(BF16)
    k2[BLOCK:, :] = jnp.where(low, ka_r, 0.0).astype(BF16)
    k3[BLOCK:, :] = jnp.where(low, 0.0, ka).astype(BF16)
    va_t = va.T
    va_tr = pltpu.roll(va_t, HEAD_DIM_A, 0)
    top = lax.broadcasted_iota(jnp.int32, (KA, T), 0) < HEAD_DIM_A
    vt0[:, BLOCK:] = jnp.where(top, va_t, 0.0).astype(BF16)
    vt1[:, BLOCK:] = jnp.where(top, 0.0, va_tr).astype(BF16)
    vt2[:, BLOCK:] = jnp.where(top, va_tr, 0.0).astype(BF16)
    vt3[:, BLOCK:] = jnp.where(top, 0.0, va_t).astype(BF16)

    first_slab = lax.broadcasted_iota(jnp.int32, (1, 2 * BLOCK), 1) < BLOCK
    sink_row = []
    for g in range(N_KV_A):
        for e in range(2):
            sink_row.append(jnp.where(first_slab, sinks_ref[layer, 4 * g + e], sinks_ref[layer, 4 * g + 2 + e]))
    top_o = lax.broadcasted_iota(jnp.int32, (KA, 2 * BLOCK), 0) < HEAD_DIM_A

    def scores(c):
        out = []
        for g in range(N_KV_A):
            qg_t = qt_s[g, :, c * 2 * BLOCK:(c + 1) * 2 * BLOCK]
            for e in range(2):
                out.append(_dot(kbufs[2 * g + e][c * BLOCK:(c + 2) * BLOCK, :], qg_t))
        return out

    def softmax(c, s_list):
        for ge in range(4):
            if c == 0:
                bias = bias_ref[jnp.where(t == 0, 4, 0) + ge]
            else:
                bias = bias_ref[ge]
            s = s_list[ge] + bias
            m = jnp.maximum(jnp.max(s, axis=0, keepdims=True), sink_row[ge])
            pe = jnp.exp(s - m)
            den = jnp.sum(pe, axis=0, keepdims=True) + jnp.exp(sink_row[ge] - m)
            p_s[c * 4 + ge] = pe.astype(BF16)
            inv_s[c * 4 + ge] = jnp.broadcast_to(1.0 / den, (8, 2 * BLOCK))

    def attend(c):
        for g in range(N_KV_A):
            o_t = (_dot(vtbufs[2 * g][:, c * BLOCK:(c + 2) * BLOCK], p_s[c * 4 + 2 * g])
                   + _dot(vtbufs[2 * g + 1][:, c * BLOCK:(c + 2) * BLOCK], p_s[c * 4 + 2 * g + 1]))
            o_t = o_t * jnp.where(top_o, inv_s[c * 4 + 2 * g][0:1, :], inv_s[c * 4 + 2 * g + 1][0:1, :])
            oa_s[c * BLOCK:(c + 1) * BLOCK, 2 * g * KA:(2 * g + 1) * KA] = o_t[:, 0:BLOCK].T.astype(BF16)
            oa_s[c * BLOCK:(c + 1) * BLOCK, (2 * g + 1) * KA:(2 * g + 2) * KA] = o_t[:, BLOCK:].T.astype(BF16)

    RC = RET_CHUNK
    NR = T // RC
    g_chunk = [math.exp(RC * lg) for lg in LOG_GAMMA]

    def ret_local(zr, vr, cc):
        sc, upd = [], []
        for h in range(N_HEADS_R):
            qh = zr[cc * RC:(cc + 1) * RC, DK_R * h:DK_R * (h + 1)].astype(BF16)
            kf = zr[cc * RC:(cc + 1) * RC, QR + DK_R * h:QR + DK_R * (h + 1)] * (DK_R ** -0.5)
            vh = vr[cc * RC:(cc + 1) * RC, DV_R * h:DV_R * (h + 1)]
            sc.append(_dot_nt(qh, kf.astype(BF16)))
            upd.append(_dot((kf * kdec_ref[h]).T.astype(BF16), vh))
        return sc, upd

    def ret_out(zr, vr, gr, cc, sc, s_in):
        for h in range(N_HEADS_R):
            qh = zr[cc * RC:(cc + 1) * RC, DK_R * h:DK_R * (h + 1)].astype(BF16)
            vh = vr[cc * RC:(cc + 1) * RC, DV_R * h:DV_R * (h + 1)]
            o = (_dot((sc[h] * dec_ref[h]).astype(BF16), vh)
                 + _dot(qh, s_in[h].astype(BF16)) * rowdec_ref[h])
            mu = jnp.mean(o, axis=-1, keepdims=True)
            d = o - mu
            var = jnp.mean(d * d, axis=-1, keepdims=True)
            on = d * lax.rsqrt(var + GN_EPS)
            gate = _silu(gr[cc * RC:(cc + 1) * RC, DV_R * h:DV_R * (h + 1)])
            orf_s[cc * RC:(cc + 1) * RC, DV_R * h:DV_R * (h + 1)] = (
                on * gn_ref[0:1, DV_R * h:DV_R * (h + 1)] * gate).astype(BF16)

    s_all = [scores(c) for c in range(NB)]
    ga_s[...] = jax.nn.sigmoid(_dot(xb, win_ref[:, C_GATE:C_GATE + D_MODEL]))
    for c in range(NB):
        softmax(c, s_all[c])
    gr_s[...] = jax.nn.sigmoid(_dot(xb, win_ref[:, C_GATE + D_MODEL:IN_COLS]))
    zr = _dot(xb, win_ref[:, C_QR:C_VR])
    vr = _dot(xb, win_ref[:, C_VR:C_GR]).astype(BF16)
    gr = _dot(xb, win_ref[:, C_GR:C_GATE])
    for c in range(NB):
        attend(c)
    for buf in vtbufs:
        buf[:, 0:BLOCK] = buf[:, T:T + BLOCK]
    for buf in kbufs:
        buf[0:BLOCK, :] = buf[T:T + BLOCK, :]

    local = [ret_local(zr, vr, cc) for cc in range(NR)]
    states = [[state_ref[h] for h in range(N_HEADS_R)]]
    for cc in range(NR):
        states.append([g_chunk[h] * states[cc][h] + local[cc][1][h] for h in range(N_HEADS_R)])
    for h in range(N_HEADS_R):
        state_ref[h] = states[NR][h]
    ya = _dot(oa_s[...], wa_ref[...])
    for cc in range(NR):
        rows = slice(cc * RC, (cc + 1) * RC)
        ret_out(zr, vr, gr, cc, local[cc][0], states[cc])
        merged = ga_s[rows, :] * ya[rows, :] + gr_s[rows, :] * _dot(orf_s[rows, :], wr_ref[...])
        y = _dot(merged.astype(BF16), wo_ref[...])
        y_ref[rows, :] = _layer_norm(ALPHA * x[rows, :] + y, g_ref[0:1, :], b_ref[0:1, :])


def _attention_bias():
    i = np.arange(BLOCK)[:, None]
    j = np.arange(2 * BLOCK)[None, :]
    dist = i + BLOCK - j
    band = (dist >= 0) & (dist <= WINDOW)
    out = np.zeros((8, 2 * BLOCK, 2 * BLOCK), np.float32)
    for first in range(2):
        valid = band & ((j >= BLOCK) | (first == 0))
        for g in range(N_KV_A):
            for e in range(2):
                for half in range(2):
                    h = 4 * g + 2 * half + e
                    out[4 * first + 2 * g + e, :, half * BLOCK:(half + 1) * BLOCK] = np.where(
                        valid, -SLOPES[h] * dist, NEG).T
    return jnp.asarray(out)


def _retention_tables():
    pos = np.arange(RET_CHUNK, dtype=np.float64)
    diff = pos[:, None] - pos[None, :]
    lg = np.asarray(LOG_GAMMA)[:, None, None]
    dec = np.where(diff >= 0, np.exp(np.maximum(diff, 0.0) * lg), 0.0)
    rowdec = np.broadcast_to(np.exp((pos + 1.0)[None, :, None] * lg), (N_HEADS_R, RET_CHUNK, DV_R))
    kdec = np.broadcast_to(np.exp((RET_CHUNK - 1.0 - pos)[None, :, None] * lg),
                           (N_HEADS_R, RET_CHUNK, DK_R))
    return (jnp.asarray(dec, F32), jnp.asarray(rowdec, F32), jnp.asarray(kdec, F32))


def _mix_prompt(x, sinks, w_in, gn, wa, wr, wo, g, b, layer, smp, stacked, side=()):
    B, S, _ = x.shape
    T = MIX_TILE
    nt = S // T
    bias = _attention_bias()
    dec, rowdec, kdec = _retention_tables()
    step = lambda bi, ti: bi * nt + ti
    s_in, s_out, s_shape, s_args = _side_specs(side, step, B * nt)
    n = smp[0].shape[0]
    c_in, c_args, n_alias, c_out, c_shape = _smp_core_specs(*smp, stacked, layer, n // (B * nt), step)
    n_fixed = 13
    aliases = {n_fixed + 7 + k: 4 + 2 + k for k in range(n_alias)}
    outs = pl.pallas_call(
        functools.partial(_mix_prompt_body, layer=layer, n_side=len(side), n_alias=n_alias),
        grid=(B, nt),
        in_specs=[
            pl.BlockSpec(memory_space=pltpu.SMEM),
            pl.BlockSpec((None, T, D_MODEL), lambda bi, ti: (bi, ti, 0)),
            _whole(w_in),
            _whole(bias), _whole(dec), _whole(rowdec), _whole(kdec),
            _resident((PARAM_ROWS,VR), layer),
            _whole(wa),
            _whole(wr),
            _whole(wo),
            _resident((PARAM_ROWS,D_MODEL), layer),
            _resident((PARAM_ROWS,D_MODEL), layer),
        ] + c_in + s_in,
        out_specs=[
            pl.BlockSpec((None, T, D_MODEL), lambda bi, ti: (bi, ti, 0)),
            pl.BlockSpec((None, WINDOW, KA), lambda bi, ti: (bi, 0, 0)),
            pl.BlockSpec((None, WINDOW, KA), lambda bi, ti: (bi, 0, 0)),
            pl.BlockSpec((None, N_HEADS_R, DK_R, DV_R), lambda bi, ti: (bi, 0, 0, 0)),
        ] + c_out + s_out,
        out_shape=[
            jax.ShapeDtypeStruct((B, S, D_MODEL), F32),
            jax.ShapeDtypeStruct((B, WINDOW, KA), F32),
            jax.ShapeDtypeStruct((B, WINDOW, KA), F32),
            jax.ShapeDtypeStruct((B, N_HEADS_R, DK_R, DV_R), F32),
        ] + c_shape + s_shape,
        input_output_aliases=aliases,
        scratch_shapes=(
            [pltpu.VMEM((N_KV_A, KA, T * 2), BF16)]
            + [pltpu.VMEM((KA, T + BLOCK), BF16) for _ in range(4)]
            + [pltpu.VMEM((T + BLOCK, KA), BF16) for _ in range(4)]
            + [pltpu.VMEM((T, QA), BF16), pltpu.VMEM((T, VR), BF16),
               pltpu.VMEM((T // BLOCK * 4, 2 * BLOCK, 2 * BLOCK), BF16),
               pltpu.VMEM((T // BLOCK * 4, 8, 2 * BLOCK), F32),
               pltpu.VMEM((T, D_MODEL), F32), pltpu.VMEM((T, D_MODEL), F32)]),
        compiler_params=pltpu.CompilerParams(
            dimension_semantics=("arbitrary", "arbitrary"), vmem_limit_bytes=VMEM_LIMIT),
        name="mix_prompt",
    )(sinks, x, w_in, bias, dec, rowdec, kdec, gn, wa, wr, wo, g, b, *c_args, *s_args)
    return outs[0], outs[1], outs[2], outs[3], outs[4:9], outs[9:]


def _head_placement():
    e = np.zeros((QA, N_HEADS_A * KA), np.float32)
    for h in range(N_HEADS_A):
        g = h // GROUP_A
        for d in range(HEAD_DIM_A):
            e[h * HEAD_DIM_A + d, h * KA + g * HEAD_DIM_A + d] = 1.0
    return e


def _smp_proj_body(x_ref, win_ref, e_ref, qm_ref, kvn_ref, zt_ref, vr_ref, zg_ref):
    z = _dot(x_ref[...].astype(BF16), win_ref[...])
    qa = (z[:, 0:QA] * (HEAD_DIM_A ** -0.5)).astype(BF16)
    qm_ref[...] = _dot(qa, e_ref[...])
    kvn_ref[...] = z[:, C_KA:C_QR]
    zt_ref[...] = z[:, C_QR:C_VR].T
    vr_ref[...] = z[:, C_VR:C_GR]
    zg_ref[...] = z[:, C_GR:]


def _smp_proj(x, w_in, e, layer):
    n = x.shape[0]
    widths = [(n, N_HEADS_A * KA), (n, 2 * KA), (2 * QR, n), (n, VR), (n, IN_COLS - C_GR)]
    return pl.pallas_call(
        _smp_proj_body,
        grid=(1,),
        in_specs=[
            pl.BlockSpec((n, D_MODEL), lambda i: (0, 0)),
            _whole(w_in),
            pl.BlockSpec((QA, N_HEADS_A * KA), lambda i: (0, 0)),
        ],
        out_specs=[pl.BlockSpec(s, lambda i: (0, 0)) for s in widths],
        out_shape=[jax.ShapeDtypeStruct(s, F32) for s in widths],
        compiler_params=pltpu.CompilerParams(
            dimension_semantics=("arbitrary",), vmem_limit_bytes=VMEM_LIMIT),
        name="smp_proj",
    )(x, w_in, e)


def _smp_core_tile(sinks_ref, qm_ref, kvn_ref, ck_ref, cv_ref, zt_ref, vr_ref, st_ref,
                   o3_ref, or_ref, nk_ref, nv_ref, ns_ref, *, layer, tile):
    tb = qm_ref.shape[0]
    row8 = lax.broadcasted_iota(jnp.int32, (N_HEADS_A, KA), 0)
    lane8 = lax.broadcasted_iota(jnp.int32, (N_HEADS_A, KA), 1)
    slope = jnp.zeros((N_HEADS_A, KA), F32)
    sinkv = jnp.zeros((N_HEADS_A, KA), F32)
    for h in range(N_HEADS_A):
        slope = jnp.where(row8 == h, SLOPES[h], slope)
        sinkv = jnp.where(row8 == h, sinks_ref[layer, h], sinkv)
    sink = sinkv[:, 0:1]
    bias = slope * (WINDOW - lane8).astype(F32)
    in_group = (lane8 < HEAD_DIM_A) == (row8 < GROUP_A)
    last_row = lax.broadcasted_iota(jnp.int32, (WINDOW, KA), 0) == WINDOW - 1
    n = zt_ref.shape[1]
    zt = pltpu.roll(zt_ref[...], lax.rem(n - tile * tb, n), 1)
    scores = [_dot_nt(qm_ref[bl].astype(BF16), ck_ref[bl].astype(BF16)) for bl in range(tb)]
    probs = []
    for bl in range(tb):
        qm = qm_ref[bl]
        kn = kvn_ref[bl:bl + 1, 0:KA]
        s = scores[bl] - bias
        s_new = jnp.sum(qm * kn, axis=-1, keepdims=True)
        m = jnp.maximum(jnp.maximum(jnp.max(s, axis=-1, keepdims=True), s_new), sink)
        p = jnp.exp(s - m)
        p_new = jnp.exp(s_new - m)
        inv = 1.0 / (jnp.sum(p, axis=-1, keepdims=True) + p_new + jnp.exp(sink - m))
        probs.append(((p * inv).astype(BF16), p_new * inv))
    for bl in range(tb):
        kb = ck_ref[bl]
        vb = cv_ref[bl]
        kn = kvn_ref[bl:bl + 1, 0:KA]
        vn = kvn_ref[bl:bl + 1, KA:2 * KA]
        o = _dot(probs[bl][0], vb.astype(BF16)) + probs[bl][1] * vn
        o3_ref[bl] = jnp.where(in_group, o, 0.0)
        nk_ref[bl] = jnp.where(last_row, kn, pltpu.roll(kb, WINDOW - 1, 0))
        nv_ref[bl] = jnp.where(last_row, vn, pltpu.roll(vb, WINDOW - 1, 0))
    for bl in range(tb):
        for h in range(N_HEADS_R):
            gamma = math.exp(LOG_GAMMA[h])
            qc = zt[DK_R * h:DK_R * (h + 1), bl:bl + 1]
            kc = zt[QR + DK_R * h:QR + DK_R * (h + 1), bl:bl + 1] * (DK_R ** -0.5)
            s0 = st_ref[bl, h]
            v = vr_ref[bl:bl + 1, DV_R * h:DV_R * (h + 1)]
            sc = jnp.sum(qc * kc, axis=0, keepdims=True)
            o_r = sc * v + jnp.sum(qc * s0, axis=0, keepdims=True) * gamma
            ns_ref[bl, h] = gamma * s0 + kc * v
            or_ref[bl:bl + 1, DV_R * h:DV_R * (h + 1)] = o_r


def _smp_core_specs(qm3, kvn, ck, cv, zt, vr, st, stacked, layer, tb, step):
    n = qm3.shape[0]
    nt = n // tb
    tile3 = lambda *g: (step(*g), 0, 0)
    lay4 = lambda *g: (layer, step(*g), 0, 0)
    lay5 = lambda *g: (layer, step(*g), 0, 0, 0)
    in_specs = [
        pl.BlockSpec((tb, N_HEADS_A, KA), tile3),
        pl.BlockSpec((None, tb, 2 * KA), tile3),
        pl.BlockSpec((None, tb, WINDOW, KA), lay4),
        pl.BlockSpec((None, tb, WINDOW, KA), lay4),
        pl.BlockSpec((2 * QR, n), lambda *g: (0, 0)),
        pl.BlockSpec((None, tb, VR), tile3),
        pl.BlockSpec((None, tb, N_HEADS_R, DK_R, DV_R), lay5),
    ]
    args = [qm3, kvn.reshape(nt, tb, 2 * KA), ck, cv,
            zt, vr.reshape(nt, tb, VR), st]
    n_alias = 0
    if stacked is not None:
        n_alias = len(stacked)
        in_specs += [pl.BlockSpec(memory_space=pl.ANY)] * n_alias
        args += list(stacked)
    out_specs = [
        pl.BlockSpec((tb, N_HEADS_A, KA), tile3),
        pl.BlockSpec((None, tb, VR), tile3),
        pl.BlockSpec((None, tb, WINDOW, KA), lay4),
        pl.BlockSpec((None, tb, WINDOW, KA), lay4),
        pl.BlockSpec((None, tb, N_HEADS_R, DK_R, DV_R), lay5),
    ]
    out_shape = [
        jax.ShapeDtypeStruct((n, N_HEADS_A, KA), F32),
        jax.ShapeDtypeStruct((nt, tb, VR), F32),
        jax.ShapeDtypeStruct((DEPTH, n, WINDOW, KA), F32),
        jax.ShapeDtypeStruct((DEPTH, n, WINDOW, KA), F32),
        jax.ShapeDtypeStruct((DEPTH, n, N_HEADS_R, DK_R, DV_R), F32),
    ]
    return in_specs, args, n_alias, out_specs, out_shape


def _smp_out_body(x_ref, zg_ref, o3_ref, or_ref, et_ref, gn_ref, wa_ref, wr_ref, wo_ref,
                  g_ref, b_ref, y_ref):
    x = x_ref[...]
    oa = _dot(o3_ref[...].astype(BF16), et_ref[...]).astype(BF16)
    gr = zg_ref[:, 0:VR]
    parts = []
    for h in range(N_HEADS_R):
        o = or_ref[:, DV_R * h:DV_R * (h + 1)]
        mu = jnp.mean(o, axis=-1, keepdims=True)
        d = o - mu
        var = jnp.mean(d * d, axis=-1, keepdims=True)
        parts.append(d * lax.rsqrt(var + GN_EPS))
    orn = jnp.concatenate(parts, axis=-1)
    orf = (orn * gn_ref[0:1, :] * _silu(gr)).astype(BF16)
    merged = (jax.nn.sigmoid(zg_ref[:, VR:VR + D_MODEL]) * _dot(oa, wa_ref[...])
              + jax.nn.sigmoid(zg_ref[:, VR + D_MODEL:]) * _dot(orf, wr_ref[...]))
    y = _dot(merged.astype(BF16), wo_ref[...])
    y_ref[...] = _layer_norm(ALPHA * x + y, g_ref[0:1, :], b_ref[0:1, :])


def _smp_out(x, zg, o3, o_r, et, gn, wa, wr, wo, g, b, layer):
    n = x.shape[0]
    whole = lambda a: pl.BlockSpec(a.shape, lambda i: (0,) * a.ndim)
    return pl.pallas_call(
        _smp_out_body,
        grid=(1,),
        in_specs=[
            whole(x), whole(zg), whole(o3), whole(o_r), whole(et),
            _resident((PARAM_ROWS,VR), layer),
            _whole(wa),
            _whole(wr),
            _whole(wo),
            _resident((PARAM_ROWS,D_MODEL), layer),
            _resident((PARAM_ROWS,D_MODEL), layer),
        ],
        out_specs=pl.BlockSpec((n, D_MODEL), lambda i: (0, 0)),
        out_shape=jax.ShapeDtypeStruct((n, D_MODEL), F32),
        compiler_params=pltpu.CompilerParams(
            dimension_semantics=("arbitrary",), vmem_limit_bytes=VMEM_LIMIT),
        name="smp_out",
    )(x, zg, o3, o_r, et, gn, wa, wr, wo, g, b)


def kernel(x_prompt, x_sample, cache_win_k, cache_win_v, state_ret, w_ff1_gu, w_ff1_dn, ln1_g, ln1_b, w_in, attn_sinks, ret_gn_g, w_br_a, w_br_r, w_o, ln2_g, ln2_b, w_ff2_gu, w_ff2_dn, ln3_g, ln3_b):
    B, S, _ = x_prompt.shape
    n = x_sample.shape[0]
    row = lambda p: jnp.broadcast_to(p.reshape(DEPTH, 1, -1), (DEPTH, PARAM_ROWS, p.shape[-1]))
    g1, b1, g2, b2, g3, b3 = row(ln1_g), row(ln1_b), row(ln2_g), row(ln2_b), row(ln3_g), row(ln3_b)
    gn = row(ret_gn_g)
    e_np = _head_placement()
    e = jnp.asarray(e_np, BF16)
    et = jnp.asarray(e_np.T, BF16)
    ck = cache_win_k.reshape(DEPTH, n, WINDOW, KA)
    cv = cache_win_v.reshape(DEPTH, n, WINDOW, KA)

    xp = x_prompt.reshape(B * S, D_MODEL)
    xs = x_sample.reshape(n, D_MODEL)
    pk, pv, pr = [], [], []
    stacked = None
    ffn_steps = B * S // FFN_TILE
    mix_steps = B * S // MIX_TILE
    ffn1_w = (w_ff1_gu[0].astype(BF16), w_ff1_dn[0].astype(BF16))
    for l in range(DEPTH):
        xp, xs, (win, wa, wr, wo) = _ffn(
            xp, xs, *ffn1_w, g1, b1, l, FFN_TILE,
            side=[_side(w, l, ffn_steps) for w in (w_in, w_br_a, w_br_r, w_o)])
        qm, kvn, zt, vr, zg = _smp_proj(xs, win, e, l)
        smp = (qm.reshape(n, N_HEADS_A, KA), kvn, ck, cv, zt, vr, state_ret)
        yp, kw, vw, st, (o3, o_r, *stacked), ffn2_w = _mix_prompt(
            xp.reshape(B, S, D_MODEL), attn_sinks, win, gn, wa, wr, wo, g2, b2, l, smp, stacked,
            side=[_side(w, l, mix_steps) for w in (w_ff2_gu, w_ff2_dn)])
        xp = yp.reshape(B * S, D_MODEL)
        pk.append(kw)
        pv.append(vw)
        pr.append(st)
        xs = _smp_out(xs, zg, o3.reshape(n, N_HEADS_A * KA), o_r.reshape(n, VR), et, gn, wa, wr, wo,
                      g2, b2, l)
        nxt = [_side(w, l + 1, ffn_steps) for w in (w_ff1_gu, w_ff1_dn)] if l + 1 < DEPTH else []
        xp, xs, ffn1_w = _ffn(xp, xs, *ffn2_w, g3, b3, l, FFN_TILE, side=nxt)
    win_shape = (DEPTH, -1, WINDOW, N_KV_A, HEAD_DIM_A)
    return (xp.reshape(B, S, D_MODEL), xs.reshape(n, 1, D_MODEL),
            jnp.stack(pk).reshape(win_shape), jnp.stack(pv).reshape(win_shape), jnp.stack(pr),
            stacked[0].reshape(win_shape), stacked[1].reshape(win_shape), stacked[2])
```

```python
import functools
import math

import numpy as np
import jax
import jax.numpy as jnp
from jax import lax
from jax.experimental import pallas as pl
from jax.experimental.pallas import tpu as pltpu

D_MODEL = 1024
DEPTH = 4
N_HEADS_A = 8
N_KV_A = 2
HEAD_DIM_A = 64
GROUP_A = N_HEADS_A // N_KV_A
WINDOW = 128
BLOCK = 128
N_HEADS_R = 4
DK_R = 128
DV_R = 256
D_FF = 2816
ALPHA = (2.0 * DEPTH) ** 0.25
LN_EPS = 1e-5
GN_EPS = 1e-6
NEG = -1e30

QA = N_HEADS_A * HEAD_DIM_A
KA = N_KV_A * HEAD_DIM_A
QR = N_HEADS_R * DK_R
VR = N_HEADS_R * DV_R
C_QA, C_KA, C_VA = 0, QA, QA + KA
C_QR = QA + 2 * KA
C_KR = C_QR + QR
C_VR = C_KR + QR
C_GR = C_VR + VR
C_GATE = C_GR + VR
IN_COLS = C_GATE + 2 * D_MODEL

SLOPES = [2.0 ** (-8.0 * (h + 1) / N_HEADS_A) for h in range(N_HEADS_A)]
LOG_GAMMA = [math.log1p(-(2.0 ** (-5.0 - h))) for h in range(N_HEADS_R)]

BF16 = jnp.bfloat16
F32 = jnp.float32

FFN_CHUNK = 256
FFN_TILE = 1024
FFN_SUBTILE = 512
FFN_MIN_PIECE = 256
MIX_TILE = 512
RET_CHUNK = 256
VMEM_LIMIT = 60 * 1024 * 1024


def _dot(a, b):
    return jnp.dot(a, b, preferred_element_type=F32)


def _dot_nt(a, b):
    return lax.dot_general(a, b, (((1,), (1,)), ((), ())), preferred_element_type=F32)


def _layer_norm(v, g, b):
    mu = jnp.mean(v, axis=-1, keepdims=True)
    d = v - mu
    var = jnp.mean(d * d, axis=-1, keepdims=True)
    return d * lax.rsqrt(var + LN_EPS) * g + b


def _silu(a):
    return a * jax.nn.sigmoid(a)


def _cast_side(refs, n_side):
    for k in range(n_side):
        refs[n_side + k][...] = refs[k][...].astype(BF16)


BF16_SUBLANES = 16


def _side(w, layer, total):
    r = w.shape[1]
    nblk = total
    while r % nblk or (r // nblk) % BF16_SUBLANES:
        nblk //= 2
    return (w, layer, r // nblk)


def _side_specs(side, step, total):
    in_specs, out_specs, out_shape, args = [], [], [], []
    for w, layer, rows in side:
        _, r, c = w.shape
        nblk = r // rows
        assert r % rows == 0 and total % nblk == 0
        idx = lambda *g, nblk=nblk: step(*g) // (total // nblk)
        in_specs.append(pl.BlockSpec((None, rows, c), lambda *g, idx=idx, layer=layer: (layer, idx(*g), 0)))
        out_specs.append(pl.BlockSpec((rows, c), lambda *g, idx=idx: (idx(*g), 0)))
        out_shape.append(jax.ShapeDtypeStruct((r, c), BF16))
        args.append(w)
    return in_specs, out_specs, out_shape, args


def _down_pieces(tm):
    pieces, left = [], tm
    while left > 2 * FFN_MIN_PIECE:
        pieces.append(left // 2)
        left -= left // 2
    return pieces + ([left // 2, left // 2] if left > FFN_MIN_PIECE else [left])


def _ffn_body(x_ref, xs_ref, wgu_ref, wdn_ref, g_ref, b_ref, *rest, n_side):
    side_in, rest = rest[:n_side], rest[n_side:]
    (o_ref, os_ref), rest = rest[:2], rest[2:]
    side_out, (h_ref,) = rest[:n_side], rest[n_side:]
    _cast_side(side_in + side_out, n_side)

    @pl.when(pl.program_id(0) == 0)
    def _():
        _ffn_rows(xs_ref, os_ref, h_ref, wgu_ref, wdn_ref, g_ref, b_ref)

    _ffn_rows(x_ref, o_ref, h_ref, wgu_ref, wdn_ref, g_ref, b_ref)


def _ffn_rows(x_ref, o_ref, h_ref, wgu_ref, wdn_ref, g_ref, b_ref):
    tm = x_ref.shape[0]
    sub = min(FFN_SUBTILE, tm)
    for s in range(tm // sub):
        rows = slice(s * sub, (s + 1) * sub)
        xb = x_ref[rows, :].astype(BF16)
        for j in range(D_FF // FFN_CHUNK):
            lo = j * FFN_CHUNK
            a = _dot(xb, wgu_ref[:, lo:lo + FFN_CHUNK])
            u = _dot(xb, wgu_ref[:, D_FF + lo:D_FF + lo + FFN_CHUNK])
            h_ref[rows, lo:lo + FFN_CHUNK] = (_silu(a) * u).astype(BF16)
    start = 0
    for size in _down_pieces(tm):
        rows = slice(start, start + size)
        d = _dot(h_ref[rows, :], wdn_ref[...])
        o_ref[rows, :] = _layer_norm(ALPHA * x_ref[rows, :] + 0.5 * d, g_ref[...], b_ref[...])
        start += size


def _resident(shape, layer):
    nd = len(shape)
    return pl.BlockSpec((None,) + shape, lambda *_: (layer,) + (0,) * nd,
                        pipeline_mode=pl.Buffered(1))


def _whole(a):
    return pl.BlockSpec(a.shape, lambda *_: (0,) * a.ndim, pipeline_mode=pl.Buffered(1))


def _ffn(x, xs, wgu, wdn, g, b, layer, tm, side=()):
    m = x.shape[0]
    n = xs.shape[0]
    steps = m // tm
    s_in, s_out, s_shape, s_args = _side_specs(side, lambda i: i, steps)
    outs = pl.pallas_call(
        functools.partial(_ffn_body, n_side=len(side)),
        grid=(steps,),
        in_specs=[
            pl.BlockSpec((tm, D_MODEL), lambda i: (i, 0)),
            pl.BlockSpec((n, D_MODEL), lambda i: (0, 0)),
            _whole(wgu),
            _whole(wdn),
            _resident((1, D_MODEL), layer),
            _resident((1, D_MODEL), layer),
        ] + s_in,
        out_specs=[pl.BlockSpec((tm, D_MODEL), lambda i: (i, 0)),
                   pl.BlockSpec((n, D_MODEL), lambda i: (0, 0))] + s_out,
        out_shape=[jax.ShapeDtypeStruct((m, D_MODEL), F32),
                   jax.ShapeDtypeStruct((n, D_MODEL), F32)] + s_shape,
        scratch_shapes=[pltpu.VMEM((tm, D_FF), BF16)],
        compiler_params=pltpu.CompilerParams(
            dimension_semantics=("arbitrary",), vmem_limit_bytes=VMEM_LIMIT),
        name="ffn",
    )(x, xs, wgu, wdn, g, b, *s_args)
    return outs[0], outs[1], outs[2:]


def _mix_prompt_body(sinks_ref, x_ref, win_ref, bias_ref, dec_ref, rowdec_ref, kdec_ref, gn_ref,
                     wa_ref, wr_ref, wo_ref, g_ref, b_ref, *rest, layer, n_side, n_alias):
    smp_in, rest = rest[:7], rest[7 + n_alias:]
    side_in, rest = rest[:n_side], rest[n_side:]
    (y_ref, kwin_ref, vwin_ref, state_ref), rest = rest[:4], rest[4:]
    smp_out, rest = rest[:5], rest[5:]
    side_out, rest = rest[:n_side], rest[n_side:]
    qt_s, vt0, vt1, vt2, vt3, k0, k1, k2, k3, oa_s, orf_s, p_s, inv_s, ga_s, gr_s = rest
    _mix_prompt_tile(sinks_ref, x_ref, win_ref, bias_ref, dec_ref, rowdec_ref, kdec_ref, gn_ref,
                     wa_ref, wr_ref, wo_ref, g_ref, b_ref, y_ref, kwin_ref, vwin_ref, state_ref,
                     qt_s, vt0, vt1, vt2, vt3, k0, k1, k2, k3, oa_s, orf_s, p_s, inv_s, ga_s, gr_s, layer=layer)
    _cast_side(side_in + side_out, n_side)
    tile = pl.program_id(0) * pl.num_programs(1) + pl.program_id(1)
    _smp_core_tile(sinks_ref, *smp_in, *smp_out, layer=layer, tile=tile)


def _mix_prompt_tile(sinks_ref, x_ref, win_ref, bias_ref, dec_ref, rowdec_ref, kdec_ref, gn_ref,
                     wa_ref, wr_ref, wo_ref, g_ref, b_ref,
                     y_ref, kwin_ref, vwin_ref, state_ref,
                     qt_s, vt0, vt1, vt2, vt3, k0, k1, k2, k3, oa_s, orf_s, p_s, inv_s, ga_s, gr_s,
                     *, layer):
    t = pl.program_id(1)
    T = MIX_TILE
    NB = T // BLOCK
    kbufs = (k0, k1, k2, k3)
    vtbufs = (vt0, vt1, vt2, vt3)

    @pl.when(t == 0)
    def _():
        state_ref[...] = jnp.zeros_like(state_ref)
        for buf in vtbufs:
            buf[:, 0:BLOCK] = jnp.zeros((KA, BLOCK), BF16)
        for buf in kbufs:
            buf[0:BLOCK, :] = jnp.zeros((BLOCK, KA), BF16)

    x = x_ref[...]
    xb = x.astype(BF16)

    za = _dot(xb, win_ref[:, C_QA:C_QR])
    q = za[:, 0:QA] * (HEAD_DIM_A ** -0.5)
    for c in range(NB):
        for p in range(N_HEADS_A // 2):
            col = c * 2 * BLOCK + (p % 2) * BLOCK
            qt_s[p // 2, :, col:col + BLOCK] = (
                q[c * BLOCK:(c + 1) * BLOCK, p * KA:(p + 1) * KA].T.astype(BF16))
    ka = za[:, QA:QA + KA]
    va = za[:, QA + KA:QA + 2 * KA]
    kwin_ref[...] = ka[T - WINDOW:, :]
    vwin_ref[...] = va[T - WINDOW:, :]
    ka_r = pltpu.roll(ka, HEAD_DIM_A, 1)
    low = lax.broadcasted_iota(jnp.int32, (T, KA), 1) < HEAD_DIM_A
    k0[BLOCK:, :] = jnp.where(low, ka, 0.0).astype(BF16)
    k1[BLOCK:, :] = jnp.where(low, 0.0, ka_r).astype(BF16)
    k2[BLOCK:, :] = jnp.where(low, ka_r, 0.0).astype(BF16)
    k3[BLOCK:, :] = jnp.where(low, 0.0, ka).astype(BF16)
    va_t = va.T
    va_tr = pltpu.roll(va_t, HEAD_DIM_A, 0)
    top = lax.broadcasted_iota(jnp.int32, (KA, T), 0) < HEAD_DIM_A
    vt0[:, BLOCK:] = jnp.where(top, va_t, 0.0).astype(BF16)
    vt1[:, BLOCK:] = jnp.where(top, 0.0, va_tr).astype(BF16)
    vt2[:, BLOCK:] = jnp.where(top, va_tr, 0.0).astype(BF16)
    vt3[:, BLOCK:] = jnp.where(top, 0.0, va_t).astype(BF16)

    first_slab = lax.broadcasted_iota(jnp.int32, (1, 2 * BLOCK), 1) < BLOCK
    sink_row = []
    for g in range(N_KV_A):
        for e in range(2):
            sink_row.append(jnp.where(first_slab, sinks_ref[layer, 4 * g + e], sinks_ref[layer, 4 * g + 2 + e]))
    top_o = lax.broadcasted_iota(jnp.int32, (KA, 2 * BLOCK), 0) < HEAD_DIM_A

    def scores(c):
        out = []
        for g in range(N_KV_A):
            qg_t = qt_s[g, :, c * 2 * BLOCK:(c + 1) * 2 * BLOCK]
            for e in range(2):
                out.append(_dot(kbufs[2 * g + e][c * BLOCK:(c + 2) * BLOCK, :], qg_t))
        return out

    def softmax(c, s_list):
        for ge in range(4):
            if c == 0:
                bias = bias_ref[jnp.where(t == 0, 4, 0) + ge]
            else:
                bias = bias_ref[ge]
            s = s_list[ge] + bias
            m = jnp.maximum(jnp.max(s, axis=0, keepdims=True), sink_row[ge])
            pe = jnp.exp(s - m)
            den = jnp.sum(pe, axis=0, keepdims=True) + jnp.exp(sink_row[ge] - m)
            p_s[c * 4 + ge] = pe.astype(BF16)
            inv_s[c * 4 + ge] = jnp.broadcast_to(1.0 / den, (8, 2 * BLOCK))

    def attend(c):
        for g in range(N_KV_A):
            o_t = (_dot(vtbufs[2 * g][:, c * BLOCK:(c + 2) * BLOCK], p_s[c * 4 + 2 * g])
                   + _dot(vtbufs[2 * g + 1][:, c * BLOCK:(c + 2) * BLOCK], p_s[c * 4 + 2 * g + 1]))
            o_t = o_t * jnp.where(top_o, inv_s[c * 4 + 2 * g][0:1, :], inv_s[c * 4 + 2 * g + 1][0:1, :])
            oa_s[c * BLOCK:(c + 1) * BLOCK, 2 * g * KA:(2 * g + 1) * KA] = o_t[:, 0:BLOCK].T.astype(BF16)
            oa_s[c * BLOCK:(c + 1) * BLOCK, (2 * g + 1) * KA:(2 * g + 2) * KA] = o_t[:, BLOCK:].T.astype(BF16)

    RC = RET_CHUNK
    NR = T // RC
    g_chunk = [math.exp(RC * lg) for lg in LOG_GAMMA]

    def ret_local(zr, vr, cc):
        sc, upd = [], []
        for h in range(N_HEADS_R):
            qh = zr[cc * RC:(cc + 1) * RC, DK_R * h:DK_R * (h + 1)].astype(BF16)
            kf = zr[cc * RC:(cc + 1) * RC, QR + DK_R * h:QR + DK_R * (h + 1)] * (DK_R ** -0.5)
            vh = vr[cc * RC:(cc + 1) * RC, DV_R * h:DV_R * (h + 1)]
            sc.append(_dot_nt(qh, kf.astype(BF16)))
            upd.append(_dot((kf * kdec_ref[h]).T.astype(BF16), vh))
        return sc, upd

    def ret_out(zr, vr, gr, cc, sc, s_in):
        for h in range(N_HEADS_R):
            qh = zr[cc * RC:(cc + 1) * RC, DK_R * h:DK_R * (h + 1)].astype(BF16)
            vh = vr[cc * RC:(cc + 1) * RC, DV_R * h:DV_R * (h + 1)]
            o = (_dot((sc[h] * dec_ref[h]).astype(BF16), vh)
                 + _dot(qh, s_in[h].astype(BF16)) * rowdec_ref[h])
            mu = jnp.mean(o, axis=-1, keepdims=True)
            d = o - mu
            var = jnp.mean(d * d, axis=-1, keepdims=True)
            on = d * lax.rsqrt(var + GN_EPS)
            gate = _silu(gr[cc * RC:(cc + 1) * RC, DV_R * h:DV_R * (h + 1)])
            orf_s[cc * RC:(cc + 1) * RC, DV_R * h:DV_R * (h + 1)] = (
                on * gn_ref[:, DV_R * h:DV_R * (h + 1)] * gate).astype(BF16)

    s_all = [scores(c) for c in range(NB)]
    ga_s[...] = jax.nn.sigmoid(_dot(xb, win_ref[:, C_GATE:C_GATE + D_MODEL]))
    for c in range(NB):
        softmax(c, s_all[c])
    gr_s[...] = jax.nn.sigmoid(_dot(xb, win_ref[:, C_GATE + D_MODEL:IN_COLS]))
    zr = _dot(xb, win_ref[:, C_QR:C_VR])
    vr = _dot(xb, win_ref[:, C_VR:C_GR]).astype(BF16)
    gr = _dot(xb, win_ref[:, C_GR:C_GATE])
    for c in range(NB):
        attend(c)
    for buf in vtbufs:
        buf[:, 0:BLOCK] = buf[:, T:T + BLOCK]
    for buf in kbufs:
        buf[0:BLOCK, :] = buf[T:T + BLOCK, :]

    local = [ret_local(zr, vr, cc) for cc in range(NR)]
    states = [[state_ref[h] for h in range(N_HEADS_R)]]
    for cc in range(NR):
        states.append([g_chunk[h] * states[cc][h] + local[cc][1][h] for h in range(N_HEADS_R)])
    for h in range(N_HEADS_R):
        state_ref[h] = states[NR][h]
    ya = _dot(oa_s[...], wa_ref[...])
    for cc in range(NR):
        rows = slice(cc * RC, (cc + 1) * RC)
        ret_out(zr, vr, gr, cc, local[cc][0], states[cc])
        merged = ga_s[rows, :] * ya[rows, :] + gr_s[rows, :] * _dot(orf_s[rows, :], wr_ref[...])
        y = _dot(merged.astype(BF16), wo_ref[...])
        y_ref[rows, :] = _layer_norm(ALPHA * x[rows, :] + y, g_ref[...], b_ref[...])


def _attention_bias():
    i = np.arange(BLOCK)[:, None]
    j = np.arange(2 * BLOCK)[None, :]
    dist = i + BLOCK - j
    band = (dist >= 0) & (dist <= WINDOW)
    out = np.zeros((8, 2 * BLOCK, 2 * BLOCK), np.float32)
    for first in range(2):
        valid = band & ((j >= BLOCK) | (first == 0))
        for g in range(N_KV_A):
            for e in range(2):
                for half in range(2):
                    h = 4 * g + 2 * half + e
                    out[4 * first + 2 * g + e, :, half * BLOCK:(half + 1) * BLOCK] = np.where(
                        valid, -SLOPES[h] * dist, NEG).T
    return jnp.asarray(out)


def _retention_tables():
    pos = np.arange(RET_CHUNK, dtype=np.float64)
    diff = pos[:, None] - pos[None, :]
    lg = np.asarray(LOG_GAMMA)[:, None, None]
    dec = np.where(diff >= 0, np.exp(np.maximum(diff, 0.0) * lg), 0.0)
    rowdec = np.broadcast_to(np.exp((pos + 1.0)[None, :, None] * lg), (N_HEADS_R, RET_CHUNK, DV_R))
    kdec = np.broadcast_to(np.exp((RET_CHUNK - 1.0 - pos)[None, :, None] * lg),
                           (N_HEADS_R, RET_CHUNK, DK_R))
    return (jnp.asarray(dec, F32), jnp.asarray(rowdec, F32), jnp.asarray(kdec, F32))


def _mix_prompt(x, sinks, w_in, gn, wa, wr, wo, g, b, layer, smp, stacked, side=()):
    B, S, _ = x.shape
    T = MIX_TILE
    nt = S // T
    bias = _attention_bias()
    dec, rowdec, kdec = _retention_tables()
    step = lambda bi, ti: bi * nt + ti
    s_in, s_out, s_shape, s_args = _side_specs(side, step, B * nt)
    n = smp[0].shape[0]
    c_in, c_args, n_alias, c_out, c_shape = _smp_core_specs(*smp, stacked, layer, n // (B * nt), step)
    n_fixed = 13
    aliases = {n_fixed + 7 + k: 4 + 2 + k for k in range(n_alias)}
    outs = pl.pallas_call(
        functools.partial(_mix_prompt_body, layer=layer, n_side=len(side), n_alias=n_alias),
        grid=(B, nt),
        in_specs=[
            pl.BlockSpec(memory_space=pltpu.SMEM),
            pl.BlockSpec((None, T, D_MODEL), lambda bi, ti: (bi, ti, 0)),
            _whole(w_in),
            _whole(bias), _whole(dec), _whole(rowdec), _whole(kdec),
            _resident((1, VR), layer),
            _whole(wa),
            _whole(wr),
            _whole(wo),
            _resident((1, D_MODEL), layer),
            _resident((1, D_MODEL), layer),
        ] + c_in + s_in,
        out_specs=[
            pl.BlockSpec((None, T, D_MODEL), lambda bi, ti: (bi, ti, 0)),
            pl.BlockSpec((None, WINDOW, KA), lambda bi, ti: (bi, 0, 0)),
            pl.BlockSpec((None, WINDOW, KA), lambda bi, ti: (bi, 0, 0)),
            pl.BlockSpec((None, N_HEADS_R, DK_R, DV_R), lambda bi, ti: (bi, 0, 0, 0)),
        ] + c_out + s_out,
        out_shape=[
            jax.ShapeDtypeStruct((B, S, D_MODEL), F32),
            jax.ShapeDtypeStruct((B, WINDOW, KA), F32),
            jax.ShapeDtypeStruct((B, WINDOW, KA), F32),
            jax.ShapeDtypeStruct((B, N_HEADS_R, DK_R, DV_R), F32),
        ] + c_shape + s_shape,
        input_output_aliases=aliases,
        scratch_shapes=(
            [pltpu.VMEM((N_KV_A, KA, T * 2), BF16)]
            + [pltpu.VMEM((KA, T + BLOCK), BF16) for _ in range(4)]
            + [pltpu.VMEM((T + BLOCK, KA), BF16) for _ in range(4)]
            + [pltpu.VMEM((T, QA), BF16), pltpu.VMEM((T, VR), BF16),
               pltpu.VMEM((T // BLOCK * 4, 2 * BLOCK, 2 * BLOCK), BF16),
               pltpu.VMEM((T // BLOCK * 4, 8, 2 * BLOCK), F32),
               pltpu.VMEM((T, D_MODEL), F32), pltpu.VMEM((T, D_MODEL), F32)]),
        compiler_params=pltpu.CompilerParams(
            dimension_semantics=("arbitrary", "arbitrary"), vmem_limit_bytes=VMEM_LIMIT),
        name="mix_prompt",
    )(sinks, x, w_in, bias, dec, rowdec, kdec, gn, wa, wr, wo, g, b, *c_args, *s_args)
    return outs[0], outs[1], outs[2], outs[3], outs[4:9], outs[9:]


def _head_placement():
    e = np.zeros((QA, N_HEADS_A * KA), np.float32)
    for h in range(N_HEADS_A):
        g = h // GROUP_A
        for d in range(HEAD_DIM_A):
            e[h * HEAD_DIM_A + d, h * KA + g * HEAD_DIM_A + d] = 1.0
    return e


def _smp_proj_body(x_ref, win_ref, e_ref, qm_ref, kvn_ref, zt_ref, vr_ref, zg_ref):
    z = _dot(x_ref[...].astype(BF16), win_ref[...])
    qa = (z[:, 0:QA] * (HEAD_DIM_A ** -0.5)).astype(BF16)
    qm_ref[...] = _dot(qa, e_ref[...])
    kvn_ref[...] = z[:, C_KA:C_QR]
    zt_ref[...] = z[:, C_QR:C_VR].T
    vr_ref[...] = z[:, C_VR:C_GR]
    zg_ref[...] = z[:, C_GR:]


def _smp_proj(x, w_in, e, layer):
    n = x.shape[0]
    widths = [(n, N_HEADS_A * KA), (n, 2 * KA), (2 * QR, n), (n, VR), (n, IN_COLS - C_GR)]
    return pl.pallas_call(
        _smp_proj_body,
        grid=(1,),
        in_specs=[
            pl.BlockSpec((n, D_MODEL), lambda i: (0, 0)),
            _whole(w_in),
            pl.BlockSpec((QA, N_HEADS_A * KA), lambda i: (0, 0)),
        ],
        out_specs=[pl.BlockSpec(s, lambda i: (0, 0)) for s in widths],
        out_shape=[jax.ShapeDtypeStruct(s, F32) for s in widths],
        compiler_params=pltpu.CompilerParams(
            dimension_semantics=("arbitrary",), vmem_limit_bytes=VMEM_LIMIT),
        name="smp_proj",
    )(x, w_in, e)


def _smp_core_tile(sinks_ref, qm_ref, kvn_ref, ck_ref, cv_ref, zt_ref, vr_ref, st_ref,
                   o3_ref, or_ref, nk_ref, nv_ref, ns_ref, *, layer, tile):
    tb = qm_ref.shape[0]
    row8 = lax.broadcasted_iota(jnp.int32, (N_HEADS_A, KA), 0)
    lane8 = lax.broadcasted_iota(jnp.int32, (N_HEADS_A, KA), 1)
    slope = jnp.zeros((N_HEADS_A, KA), F32)
    sinkv = jnp.zeros((N_HEADS_A, KA), F32)
    for h in range(N_HEADS_A):
        slope = jnp.where(row8 == h, SLOPES[h], slope)
        sinkv = jnp.where(row8 == h, sinks_ref[layer, h], sinkv)
    sink = sinkv[:, 0:1]
    bias = slope * (WINDOW - lane8).astype(F32)
    in_group = (lane8 < HEAD_DIM_A) == (row8 < GROUP_A)
    last_row = lax.broadcasted_iota(jnp.int32, (WINDOW, KA), 0) == WINDOW - 1
    n = zt_ref.shape[1]
    zt = pltpu.roll(zt_ref[...], lax.rem(n - tile * tb, n), 1)
    scores = [_dot_nt(qm_ref[bl].astype(BF16), ck_ref[bl].astype(BF16)) for bl in range(tb)]
    probs = []
    for bl in range(tb):
        qm = qm_ref[bl]
        kn = kvn_ref[bl:bl + 1, 0:KA]
        s = scores[bl] - bias
        s_new = jnp.sum(qm * kn, axis=-1, keepdims=True)
        m = jnp.maximum(jnp.maximum(jnp.max(s, axis=-1, keepdims=True), s_new), sink)
        p = jnp.exp(s - m)
        p_new = jnp.exp(s_new - m)
        inv = 1.0 / (jnp.sum(p, axis=-1, keepdims=True) + p_new + jnp.exp(sink - m))
        probs.append(((p * inv).astype(BF16), p_new * inv))
    for bl in range(tb):
        kb = ck_ref[bl]
        vb = cv_ref[bl]
        kn = kvn_ref[bl:bl + 1, 0:KA]
        vn = kvn_ref[bl:bl + 1, KA:2 * KA]
        o = _dot(probs[bl][0], vb.astype(BF16)) + probs[bl][1] * vn
        o3_ref[bl] = jnp.where(in_group, o, 0.0)
        nk_ref[bl] = jnp.where(last_row, kn, pltpu.roll(kb, WINDOW - 1, 0))
        nv_ref[bl] = jnp.where(last_row, vn, pltpu.roll(vb, WINDOW - 1, 0))
    for bl in range(tb):
        for h in range(N_HEADS_R):
            gamma = math.exp(LOG_GAMMA[h])
            qc = zt[DK_R * h:DK_R * (h + 1), bl:bl + 1]
            kc = zt[QR + DK_R * h:QR + DK_R * (h + 1), bl:bl + 1] * (DK_R ** -0.5)
            s0 = st_ref[bl, h]
            v = vr_ref[bl:bl + 1, DV_R * h:DV_R * (h + 1)]
            sc = jnp.sum(qc * kc, axis=0, keepdims=True)
            o_r = sc * v + jnp.sum(qc * s0, axis=0, keepdims=True) * gamma
            ns_ref[bl, h] = gamma * s0 + kc * v
            or_ref[bl:bl + 1, DV_R * h:DV_R * (h + 1)] = o_r


def _smp_core_specs(qm3, kvn, ck, cv, zt, vr, st, stacked, layer, tb, step):
    n = qm3.shape[0]
    nt = n // tb
    tile3 = lambda *g: (step(*g), 0, 0)
    lay4 = lambda *g: (layer, step(*g), 0, 0)
    lay5 = lambda *g: (layer, step(*g), 0, 0, 0)
    in_specs = [
        pl.BlockSpec((tb, N_HEADS_A, KA), tile3),
        pl.BlockSpec((None, tb, 2 * KA), tile3),
        pl.BlockSpec((None, tb, WINDOW, KA), lay4),
        pl.BlockSpec((None, tb, WINDOW, KA), lay4),
        pl.BlockSpec((2 * QR, n), lambda *g: (0, 0)),
        pl.BlockSpec((None, tb, VR), tile3),
        pl.BlockSpec((None, tb, N_HEADS_R, DK_R, DV_R), lay5),
    ]
    args = [qm3, kvn.reshape(nt, tb, 2 * KA), ck, cv,
            zt, vr.reshape(nt, tb, VR), st]
    n_alias = 0
    if stacked is not None:
        n_alias = len(stacked)
        in_specs += [pl.BlockSpec(memory_space=pl.ANY)] * n_alias
        args += list(stacked)
    out_specs = [
        pl.BlockSpec((tb, N_HEADS_A, KA), tile3),
        pl.BlockSpec((None, tb, VR), tile3),
        pl.BlockSpec((None, tb, WINDOW, KA), lay4),
        pl.BlockSpec((None, tb, WINDOW, KA), lay4),
        pl.BlockSpec((None, tb, N_HEADS_R, DK_R, DV_R), lay5),
    ]
    out_shape = [
        jax.ShapeDtypeStruct((n, N_HEADS_A, KA), F32),
        jax.ShapeDtypeStruct((nt, tb, VR), F32),
        jax.ShapeDtypeStruct((DEPTH, n, WINDOW, KA), F32),
        jax.ShapeDtypeStruct((DEPTH, n, WINDOW, KA), F32),
        jax.ShapeDtypeStruct((DEPTH, n, N_HEADS_R, DK_R, DV_R), F32),
    ]
    return in_specs, args, n_alias, out_specs, out_shape


def _smp_out_body(x_ref, zg_ref, o3_ref, or_ref, et_ref, gn_ref, wa_ref, wr_ref, wo_ref,
                  g_ref, b_ref, y_ref):
    x = x_ref[...]
    oa = _dot(o3_ref[...].astype(BF16), et_ref[...]).astype(BF16)
    gr = zg_ref[:, 0:VR]
    parts = []
    for h in range(N_HEADS_R):
        o = or_ref[:, DV_R * h:DV_R * (h + 1)]
        mu = jnp.mean(o, axis=-1, keepdims=True)
        d = o - mu
        var = jnp.mean(d * d, axis=-1, keepdims=True)
        parts.append(d * lax.rsqrt(var + GN_EPS))
    orn = jnp.concatenate(parts, axis=-1)
    orf = (orn * gn_ref[...] * _silu(gr)).astype(BF16)
    merged = (jax.nn.sigmoid(zg_ref[:, VR:VR + D_MODEL]) * _dot(oa, wa_ref[...])
              + jax.nn.sigmoid(zg_ref[:, VR + D_MODEL:]) * _dot(orf, wr_ref[...]))
    y = _dot(merged.astype(BF16), wo_ref[...])
    y_ref[...] = _layer_norm(ALPHA * x + y, g_ref[...], b_ref[...])


def _smp_out(x, zg, o3, o_r, et, gn, wa, wr, wo, g, b, layer):
    n = x.shape[0]
    whole = lambda a: pl.BlockSpec(a.shape, lambda i: (0,) * a.ndim)
    return pl.pallas_call(
        _smp_out_body,
        grid=(1,),
        in_specs=[
            whole(x), whole(zg), whole(o3), whole(o_r), whole(et),
            _resident((1, VR), layer),
            _whole(wa),
            _whole(wr),
            _whole(wo),
            _resident((1, D_MODEL), layer),
            _resident((1, D_MODEL), layer),
        ],
        out_specs=pl.BlockSpec((n, D_MODEL), lambda i: (0, 0)),
        out_shape=jax.ShapeDtypeStruct((n, D_MODEL), F32),
        compiler_params=pltpu.CompilerParams(
            dimension_semantics=("arbitrary",), vmem_limit_bytes=VMEM_LIMIT),
        name="smp_out",
    )(x, zg, o3, o_r, et, gn, wa, wr, wo, g, b)


def kernel(x_prompt, x_sample, cache_win_k, cache_win_v, state_ret, w_ff1_gu, w_ff1_dn, ln1_g, ln1_b, w_in, attn_sinks, ret_gn_g, w_br_a, w_br_r, w_o, ln2_g, ln2_b, w_ff2_gu, w_ff2_dn, ln3_g, ln3_b):
    B, S, _ = x_prompt.shape
    n = x_sample.shape[0]
    row = lambda p: p.reshape(DEPTH, 1, -1)
    g1, b1, g2, b2, g3, b3 = row(ln1_g), row(ln1_b), row(ln2_g), row(ln2_b), row(ln3_g), row(ln3_b)
    gn = row(ret_gn_g)
    e_np = _head_placement()
    e = jnp.asarray(e_np, BF16)
    et = jnp.asarray(e_np.T, BF16)
    ck = cache_win_k.reshape(DEPTH, n, WINDOW, KA)
    cv = cache_win_v.reshape(DEPTH, n, WINDOW, KA)

    xp = x_prompt.reshape(B * S, D_MODEL)
    xs = x_sample.reshape(n, D_MODEL)
    pk, pv, pr = [], [], []
    stacked = None
    ffn_steps = B * S // FFN_TILE
    mix_steps = B * S // MIX_TILE
    ffn1_w = (w_ff1_gu[0].astype(BF16), w_ff1_dn[0].astype(BF16))
    for l in range(DEPTH):
        xp, xs, (win, wa, wr, wo) = _ffn(
            xp, xs, *ffn1_w, g1, b1, l, FFN_TILE,
            side=[_side(w, l, ffn_steps) for w in (w_in, w_br_a, w_br_r, w_o)])
        qm, kvn, zt, vr, zg = _smp_proj(xs, win, e, l)
        smp = (qm.reshape(n, N_HEADS_A, KA), kvn, ck, cv, zt, vr, state_ret)
        yp, kw, vw, st, (o3, o_r, *stacked), ffn2_w = _mix_prompt(
            xp.reshape(B, S, D_MODEL), attn_sinks, win, gn, wa, wr, wo, g2, b2, l, smp, stacked,
            side=[_side(w, l, mix_steps) for w in (w_ff2_gu, w_ff2_dn)])
        xp = yp.reshape(B * S, D_MODEL)
        pk.append(kw)
        pv.append(vw)
        pr.append(st)
        xs = _smp_out(xs, zg, o3.reshape(n, N_HEADS_A * KA), o_r.reshape(n, VR), et, gn, wa, wr, wo,
                      g2, b2, l)
        nxt = [_side(w, l + 1, ffn_steps) for w in (w_ff1_gu, w_ff1_dn)] if l + 1 < DEPTH else []
        xp, xs, ffn1_w = _ffn(xp, xs, *ffn2_w, g3, b3, l, FFN_TILE, side=nxt)
    win_shape = (DEPTH, -1, WINDOW, N_KV_A, HEAD_DIM_A)
    return (xp.reshape(B, S, D_MODEL), xs.reshape(n, 1, D_MODEL),
            jnp.stack(pk).reshape(win_shape), jnp.stack(pv).reshape(win_shape), jnp.stack(pr),
            stacked[0].reshape(win_shape), stacked[1].reshape(win_shape), stacked[2])
```
